```python
import math
import jax
import jax.numpy as jnp
from jax import lax
import numpy as np

D_MODEL = 4096
BATCH = 2
SEQ = 4096
DEPTH = 2

D_MIX = D_MODEL
D_SSD = D_MIX // 2
D_GMLP = D_MIX - D_SSD
SSD_HEAD_DIM = 64
N_SSD_HEADS = D_SSD // SSD_HEAD_DIM
SSD_GROUPS = 8
D_STATE = 128
D_CONV = 5
SSD_CHUNK = 128
CONV_CH = D_SSD + 2 * SSD_GROUPS * D_STATE
GMLP_CHUNK = 128
GMLP_GROUP_WIDTH = 128
N_GMLP_GROUPS = D_GMLP // GMLP_GROUP_WIDTH
D_IN = D_SSD + CONV_CH + 2 * N_SSD_HEADS + 2 * D_GMLP
MEM_LEN = 256
XATTN_HEADS = 4
XATTN_HEAD_DIM = D_MODEL // XATTN_HEADS
N_EXPERTS = 16
CAPACITY_FACTOR = 2
D_EXPERT = 3 * D_MODEL // 8
RMS_EPS = 1e-6
DT_MIN = 1e-3
DT_MAX = 1e-1

kernel_name = "hybrid_ssd_gmlp_memxattn_ecmoe_encoder"


def rmsnorm(x, g):
    x32 = x.astype(jnp.float32)
    y = x32 * lax.rsqrt(jnp.mean(x32 * x32, axis=-1, keepdims=True) + RMS_EPS)
    return (y * g.astype(jnp.float32)).astype(x.dtype)


def depthwise_conv_centred(x, w, b):
    c = x.shape[-1]
    y = lax.conv_general_dilated(
        x, w[:, None, :].astype(x.dtype), window_strides=(1,),
        padding=[(D_CONV // 2, D_CONV // 2)],
        dimension_numbers=('NWC', 'WIO', 'NWC'), feature_group_count=c)
    return y + b.astype(x.dtype)


def ssd_scan(x, dt, a, b_mat, c_mat):
    out_dtype = x.dtype
    bsz, s, h, p = x.shape
    g, n = b_mat.shape[2], b_mat.shape[3]
    r = h // g
    nc = s // SSD_CHUNK
    f32 = jnp.float32
    xc = x.astype(f32).reshape(bsz, nc, SSD_CHUNK, g, r, p)
    dtc = dt.astype(f32).reshape(bsz, nc, SSD_CHUNK, g, r)
    bc = b_mat.astype(f32).reshape(bsz, nc, SSD_CHUNK, g, n)
    cc = c_mat.astype(f32).reshape(bsz, nc, SSD_CHUNK, g, n)
    a_cs = jnp.cumsum(dtc * a.astype(f32).reshape(g, r), axis=2)
    xdt = xc * dtc[..., None]
    lower = jnp.tril(jnp.ones((SSD_CHUNK, SSD_CHUNK), bool))[:, :, None, None]
    seg = a_cs[:, :, :, None] - a_cs[:, :, None, :]
    decay = jnp.exp(jnp.where(lower, seg, -jnp.inf))
    cb = jnp.einsum('bclgn,bcsgn->bclsg', cc, bc)
    y_diag = jnp.einsum('bclsgr,bcsgrp->bclgrp', cb[..., None] * decay, xdt)
    decay_to_end = jnp.exp(a_cs[:, :, -1:] - a_cs)
    chunk_states = jnp.einsum('bclgn,bclgr,bclgrp->bcgrpn', bc, decay_to_end, xdt)
    chunk_decay = jnp.exp(a_cs[:, :, -1])

    def step(state, inp):
        st, dec = inp
        return state * dec[..., None, None] + st, state

    init = jnp.zeros((bsz, g, r, p, n), f32)
    _, prev = lax.scan(step, init, (jnp.moveaxis(chunk_states, 1, 0), jnp.moveaxis(chunk_decay, 1, 0)))
    prev = jnp.moveaxis(prev, 0, 1)
    y_off = jnp.einsum('bclgn,bcgrpn,bclgr->bclgrp', cc, prev, jnp.exp(a_cs))
    return (y_diag + y_off).reshape(bsz, s, h, p).astype(out_dtype)


def hybrid_mixer(h, w_in, conv_w, conv_b, dt_bias, a_log, d_skip, ssd_norm_g,
                 gmlp_norm_g, gmlp_ws, gmlp_bs, w_out):
    bsz, s, _ = h.shape
    proj = jnp.einsum('bsd,de->bse', h, w_in)
    o1 = D_SSD
    o2 = o1 + CONV_CH
    o3 = o2 + 2 * N_SSD_HEADS
    o4 = o3 + D_GMLP
    z, xbc, dt_raw, u, v = proj[..., :o1], proj[..., o1:o2], proj[..., o2:o3], proj[..., o3:o4], proj[..., o4:]

    xbc = jax.nn.silu(depthwise_conv_centred(xbc, conv_w, conv_b))
    xs = xbc[..., :D_SSD].reshape(bsz, s, N_SSD_HEADS, SSD_HEAD_DIM)
    bm = xbc[..., D_SSD:D_SSD + SSD_GROUPS * D_STATE].reshape(bsz, s, SSD_GROUPS, D_STATE)
    cm = xbc[..., D_SSD + SSD_GROUPS * D_STATE:].reshape(bsz, s, SSD_GROUPS, D_STATE)
    dt = jax.nn.softplus(dt_raw.reshape(bsz, s, 2, N_SSD_HEADS).astype(jnp.float32)
                         + dt_bias.astype(jnp.float32))
    a = -jnp.exp(a_log.astype(jnp.float32))
    y_fwd = ssd_scan(xs, dt[:, :, 0], a[0], bm, cm)
    y_bwd = jnp.flip(ssd_scan(jnp.flip(xs, 1), jnp.flip(dt[:, :, 1], 1), a[1],
                              jnp.flip(bm, 1), jnp.flip(cm, 1)), 1)
    y = y_fwd + y_bwd + xs * d_skip[:, None].astype(xs.dtype)
    y = y.reshape(bsz, s, D_SSD) * jax.nn.silu(z)
    y_ssd = rmsnorm(y.reshape(bsz, s, SSD_GROUPS, D_SSD // SSD_GROUPS),
                    ssd_norm_g.reshape(SSD_GROUPS, D_SSD // SSD_GROUPS)).reshape(bsz, s, D_SSD)

    u = jax.nn.gelu(u, approximate=False)
    v = rmsnorm(jax.nn.gelu(v, approximate=False), gmlp_norm_g)
    nc = s // GMLP_CHUNK
    vc = v.reshape(bsz, nc, GMLP_CHUNK, N_GMLP_GROUPS, GMLP_GROUP_WIDTH)
    sp = jnp.einsum('gts,bcsgd->bctgd', gmlp_ws, vc) + gmlp_bs.T[None, None, :, :, None]
    y_gmlp = u * sp.reshape(bsz, s, D_GMLP)

    y_cat = jnp.concatenate([y_ssd, y_gmlp], axis=-1)
    return jnp.einsum('bse,ed->bsd', y_cat, w_out)


def memory_cross_attention(h, mem, w_q, w_kv, w_o):
    bsz, s, _ = h.shape
    m = mem.shape[1]
    q = jnp.einsum('bsd,de->bse', h, w_q).reshape(bsz, s, XATTN_HEADS, XATTN_HEAD_DIM)
    kv = jnp.einsum('bmd,de->bme', mem, w_kv)
    k = kv[..., :D_MODEL].reshape(bsz, m, XATTN_HEADS, XATTN_HEAD_DIM)
    vv = kv[..., D_MODEL:].reshape(bsz, m, XATTN_HEADS, XATTN_HEAD_DIM)
    scores = jnp.einsum('bshd,bmhd->bhsm', q, k).astype(jnp.float32) * (XATTN_HEAD_DIM ** -0.5)
    probs = jax.nn.softmax(scores, axis=-1).astype(vv.dtype)
    o = jnp.einsum('bhsm,bmhd->bshd', probs, vv).reshape(bsz, s, D_MODEL)
    return jnp.einsum('bse,ed->bsd', o, w_o)


def expert_choice_moe(h, w_router, w_gate_up, w_down):
    bsz, s, d = h.shape
    cap = CAPACITY_FACTOR * s // N_EXPERTS
    logits = jnp.einsum('bsd,de->bse', h, w_router).astype(jnp.float32)
    aff = jax.nn.softmax(logits, axis=-1)
    gate, idx = lax.top_k(jnp.swapaxes(aff, 1, 2), cap)
    xs = jax.vmap(lambda hb, ib: hb[ib])(h, idx)
    gu = jnp.einsum('becd,edf->becf', xs, w_gate_up)
    act = jax.nn.silu(gu[..., :D_EXPERT]) * gu[..., D_EXPERT:]
    y = jnp.einsum('becf,efd->becd', act, w_down) * gate[..., None].astype(h.dtype)
    return jax.vmap(lambda yb, ib: jnp.zeros((s, d), yb.dtype).at[ib.reshape(-1)].add(yb.reshape(-1, d)))(y, idx)


def setup_inputs(seed: int = 0) -> dict:
    key = jax.random.key(seed)
    ks = jax.random.split(key, 24)
    L = DEPTH
    f32 = jnp.float32

    def nrm(k, shape, scale):
        return jax.random.normal(k, shape, f32) * scale

    def gain(k, shape):
        return 1.0 + 0.05 * jax.random.normal(k, shape, f32)

    dt0 = jnp.exp(jax.random.uniform(ks[6], (L, 2, N_SSD_HEADS), f32, math.log(DT_MIN), math.log(DT_MAX)))
    dt_bias = dt0 + jnp.log(-jnp.expm1(-dt0))
    a_log = jnp.log(jax.random.uniform(ks[7], (L, 2, N_SSD_HEADS), f32, 1.0, 16.0))
    return {
        'x': nrm(ks[0], (BATCH, SEQ, D_MODEL), 1.0),
        'mem': nrm(ks[1], (BATCH, MEM_LEN, D_MODEL), 1.0),
        'norm_mix_g': gain(ks[2], (L, D_MODEL)),
        'w_in': nrm(ks[3], (L, D_MODEL, D_IN), D_MODEL ** -0.5),
        'conv_w': nrm(ks[4], (L, D_CONV, CONV_CH), D_CONV ** -0.5),
        'conv_b': nrm(ks[5], (L, CONV_CH), 0.01),
        'dt_bias': dt_bias,
        'a_log': a_log,
        'd_skip': gain(ks[8], (L, N_SSD_HEADS)),
        'ssd_norm_g': gain(ks[9], (L, D_SSD)),
        'gmlp_norm_g': gain(ks[10], (L, D_GMLP)),
        'gmlp_ws': nrm(ks[11], (L, N_GMLP_GROUPS, GMLP_CHUNK, GMLP_CHUNK), GMLP_CHUNK ** -0.5),
        'gmlp_bs': 1.0 + nrm(ks[12], (L, N_GMLP_GROUPS, GMLP_CHUNK), 0.1),
        'w_out': nrm(ks[13], (L, D_MIX, D_MODEL), D_MIX ** -0.5),
        'norm_xattn_g': gain(ks[14], (L, D_MODEL)),
        'norm_mem_g': gain(ks[15], (L, D_MODEL)),
        'w_q': nrm(ks[16], (L, D_MODEL, D_MODEL), D_MODEL ** -0.5),
        'w_kv': nrm(ks[17], (L, D_MODEL, 2 * D_MODEL), D_MODEL ** -0.5),
        'w_o': nrm(ks[18], (L, D_MODEL, D_MODEL), D_MODEL ** -0.5),
        'norm_moe_g': gain(ks[19], (L, D_MODEL)),
        'w_router': nrm(ks[20], (L, D_MODEL, N_EXPERTS), D_MODEL ** -0.5),
        'w_gate_up': nrm(ks[21], (L, N_EXPERTS, D_MODEL, 2 * D_EXPERT), D_MODEL ** -0.5),
        'w_down': nrm(ks[22], (L, N_EXPERTS, D_EXPERT, D_MODEL), D_EXPERT ** -0.5),
        'final_norm_g': gain(ks[23], (D_MODEL,)),
    }


def reference(x, mem, norm_mix_g, w_in, conv_w, conv_b, dt_bias, a_log, d_skip,
              ssd_norm_g, gmlp_norm_g, gmlp_ws, gmlp_bs, w_out, norm_xattn_g,
              norm_mem_g, w_q, w_kv, w_o, norm_moe_g, w_router, w_gate_up, w_down,
              final_norm_g):
    for l in range(DEPTH):
        h = rmsnorm(x, norm_mix_g[l])
        x = x + hybrid_mixer(h, w_in[l], conv_w[l], conv_b[l], dt_bias[l], a_log[l],
                             d_skip[l], ssd_norm_g[l], gmlp_norm_g[l], gmlp_ws[l],
                             gmlp_bs[l], w_out[l])
        h = rmsnorm(x, norm_xattn_g[l])
        m = rmsnorm(mem, norm_mem_g[l])
        x = x + memory_cross_attention(h, m, w_q[l], w_kv[l], w_o[l])
        h = rmsnorm(x, norm_moe_g[l])
        x = x + expert_choice_moe(h, w_router[l], w_gate_up[l], w_down[l])
    return rmsnorm(x, final_norm_g)
```

```python
import functools

import jax
import jax.numpy as jnp
from jax import lax
from jax.experimental import pallas as pl
from jax.experimental.pallas import tpu as pltpu

F32 = jnp.float32
BF16 = jnp.bfloat16

RMS_EPS = 1e-6
SSD_HEAD_DIM = 64
SSD_GROUPS = 8
D_STATE = 128
D_CONV = 5
CHUNK = 128
GMLP_GROUP_WIDTH = 128
XATTN_HEADS = 4
N_EXPERTS = 16
CAPACITY_FACTOR = 2
LANE = 128
VMEM_LIMIT = 56 * 1024 * 1024


def _cparams(sem, vmem=VMEM_LIMIT):
    return pltpu.CompilerParams(dimension_semantics=sem, vmem_limit_bytes=vmem)


def _silu(x):
    return x * (1.0 / (1.0 + jnp.exp(-x)))


def _rms_rows(x, g):
    ms = jnp.mean(x * x, axis=-1, keepdims=True)
    return x * lax.rsqrt(ms + RMS_EPS) * g


def _norm_rows_to(h_ref, x_ref, g_ref, rows):
    tm = x_ref.shape[0]

    def body(i, c):
        r = pl.multiple_of(i * rows, rows)
        h_ref[pl.ds(r, rows), :] = _rms_rows(x_ref[pl.ds(r, rows), :], g_ref[...]).astype(h_ref.dtype)
        return c

    lax.fori_loop(0, tm // rows, body, 0)


def _norm_mm_kernel(x_ref, g_ref, w_ref, o_ref, h_ref):
    @pl.when(pl.program_id(1) == 0)
    def _():
        _norm_rows_to(h_ref, x_ref, g_ref, 32)

    o_ref[...] = jnp.dot(h_ref[...], w_ref[...], preferred_element_type=F32).astype(o_ref.dtype)


def _norm_mm_side_kernel(x_ref, g_ref, w_ref, ws_ref, o_ref, os_ref, h_ref):
    @pl.when(pl.program_id(1) == 0)
    def _():
        _norm_rows_to(h_ref, x_ref, g_ref, 32)
        os_ref[...] = jnp.dot(h_ref[...], ws_ref[...], preferred_element_type=F32)

    o_ref[...] = jnp.dot(h_ref[...], w_ref[...], preferred_element_type=F32).astype(o_ref.dtype)


def norm_matmul(x, g, w, w_side=None, *, tm=512, tn=1024, out_dtype=BF16):
    m, k = x.shape
    n = w.shape[1]
    tm = min(tm, m)
    tn = min(tn, n)
    assert m % tm == 0 and n % tn == 0
    grid = (m // tm, n // tn)
    g2 = g.reshape(1, k).astype(F32)
    x_spec = pl.BlockSpec((tm, k), lambda i, j: (i, 0))
    g_spec = pl.BlockSpec((1, k), lambda i, j: (0, 0))
    w_spec = pl.BlockSpec((k, tn), lambda i, j: (0, j))
    o_spec = pl.BlockSpec((tm, tn), lambda i, j: (i, j))
    scratch = [pltpu.VMEM((tm, k), BF16)]
    if w_side is None:
        return pl.pallas_call(
            _norm_mm_kernel,
            out_shape=jax.ShapeDtypeStruct((m, n), out_dtype),
            grid=grid,
            in_specs=[x_spec, g_spec, w_spec],
            out_specs=o_spec,
            scratch_shapes=scratch,
            compiler_params=_cparams(("parallel", "arbitrary")),
            name="norm_matmul",
        )(x, g2, w)
    ns = w_side.shape[1]
    return pl.pallas_call(
        _norm_mm_side_kernel,
        out_shape=(jax.ShapeDtypeStruct((m, n), out_dtype), jax.ShapeDtypeStruct((m, ns), F32)),
        grid=grid,
        in_specs=[x_spec, g_spec, w_spec, pl.BlockSpec((k, ns), lambda i, j: (0, 0))],
        out_specs=(o_spec, pl.BlockSpec((tm, ns), lambda i, j: (i, 0))),
        scratch_shapes=scratch,
        compiler_params=_cparams(("parallel", "arbitrary")),
        name="norm_matmul_side",
    )(x, g2, w, w_side)


def _mm_res_kernel(a_ref, w_ref, r_ref, o_ref):
    o_ref[...] = r_ref[...] + jnp.dot(a_ref[...], w_ref[...], preferred_element_type=F32)


def matmul_residual(a, w, res, *, tm=1024, tn=512):
    m, k = a.shape
    n = w.shape[1]
    assert m % tm == 0 and n % tn == 0
    return pl.pallas_call(
        _mm_res_kernel,
        out_shape=jax.ShapeDtypeStruct((m, n), F32),
        grid=(m // tm, n // tn),
        in_specs=[
            pl.BlockSpec((tm, k), lambda i, j: (i, 0)),
            pl.BlockSpec((k, tn), lambda i, j: (0, j)),
            pl.BlockSpec((tm, tn), lambda i, j: (i, j)),
        ],
        out_specs=pl.BlockSpec((tm, tn), lambda i, j: (i, j)),
        compiler_params=_cparams(("parallel", "arbitrary")),
        name="matmul_residual",
    )(a, w, res)


HALO = 16


def _conv_kernel(prev_ref, cur_ref, next_ref, w_ref, b_ref, o_ref, ext_ref, *, n_seq_tiles):
    ts = cur_ref.shape[0]
    s = pl.program_id(1)
    prev = prev_ref[...].astype(F32)
    nxt = next_ref[...].astype(F32)
    ext_ref[0:HALO, :] = jnp.where(s == 0, 0.0, prev)
    ext_ref[HALO:HALO + ts, :] = cur_ref[...].astype(F32)
    ext_ref[HALO + ts:, :] = jnp.where(s == n_seq_tiles - 1, 0.0, nxt)
    acc = jnp.zeros(cur_ref.shape, F32) + b_ref[...]
    for k in range(D_CONV):
        off = HALO - D_CONV // 2 + k
        acc = acc + ext_ref[off:off + ts, :] * w_ref[k:k + 1, :]
    o_ref[...] = _silu(acc).astype(o_ref.dtype)


def conv_silu(proj, conv_w, conv_b, *, batch, seq, col0, ts=512, tc=1024):
    c = conv_w.shape[1]
    t = batch * seq
    nst = seq // ts
    hb = ts // HALO
    cb0 = col0 // tc
    assert col0 % tc == 0 and c % tc == 0 and seq % ts == 0
    last_halo = t // HALO - 1

    def prev_map(b, s, j):
        return (jnp.maximum((b * nst + s) * hb - 1, 0), cb0 + j)

    def next_map(b, s, j):
        return (jnp.minimum((b * nst + s + 1) * hb, last_halo), cb0 + j)

    return pl.pallas_call(
        functools.partial(_conv_kernel, n_seq_tiles=nst),
        out_shape=jax.ShapeDtypeStruct((t, c), BF16),
        grid=(batch, nst, c // tc),
        in_specs=[
            pl.BlockSpec((HALO, tc), prev_map),
            pl.BlockSpec((ts, tc), lambda b, s, j: (b * nst + s, cb0 + j)),
            pl.BlockSpec((HALO, tc), next_map),
            pl.BlockSpec((D_CONV, tc), lambda b, s, j: (0, j)),
            pl.BlockSpec((1, tc), lambda b, s, j: (0, j)),
        ],
        out_specs=pl.BlockSpec((ts, tc), lambda b, s, j: (b * nst + s, j)),
        scratch_shapes=[pltpu.VMEM((ts + 2 * HALO, tc), F32)],
        compiler_params=_cparams(("parallel", "parallel", "parallel")),
        name="conv_silu",
    )(proj, proj, proj, conv_w.astype(F32), conv_b.reshape(1, c).astype(F32))


def _split_hi_lo(x):
    hi = x.astype(BF16)
    lo = (x - hi.astype(F32)).astype(BF16)
    return hi, lo


def _ssd_direction(xs_ref, b_ref, c_ref, dt_ref, bias_ref, alog_ref, expand_ref, st_ref, y_ref,
                   *, reverse, col0, n_heads):
    L = CHUNK
    hp = SSD_HEAD_DIM
    heads_per_group = n_heads // SSD_GROUPS
    gw = heads_per_group * hp
    row = lax.broadcasted_iota(jnp.int32, (L, L), 0)
    col = lax.broadcasted_iota(jnp.int32, (L, L), 1)
    keep = (col >= row) if reverse else (col <= row)
    tri = jnp.where(keep, 1.0, 0.0).astype(F32)

    xraw = dt_ref[...] + bias_ref[...]
    dt = jnp.maximum(xraw, 0.0) + jnp.log1p(jnp.exp(-jnp.abs(xraw)))
    a = -jnp.exp(alog_ref[...])
    dta = dt * a
    acs = jnp.dot(tri, dta, preferred_element_type=F32, precision=lax.Precision.HIGHEST)
    acs_t = acs.T
    total = acs[0:1, :] if reverse else acs[L - 1:L, :]
    dte = jnp.exp(total - acs)
    ea = jnp.exp(acs)
    cdec = jnp.broadcast_to(jnp.exp(total), (8, LANE))

    stack = jnp.concatenate([dt, dte, ea, cdec], axis=0)
    s_hi, s_lo = _split_hi_lo(stack)
    e = expand_ref[...]
    ex = (jnp.dot(s_hi, e, preferred_element_type=F32)
          + jnp.dot(s_lo, e, preferred_element_type=F32))
    dt_x = ex[0:L]
    dte_x = ex[L:2 * L]
    ea_x = ex[2 * L:3 * L]
    cdec_x = ex[3 * L:3 * L + 1]

    lane_head = lax.broadcasted_iota(jnp.int32, (L, gw), 1) // hp
    for g in range(SSD_GROUPS):
        sl = slice(g * gw, (g + 1) * gw)
        xs_g = xs_ref[:, sl].astype(F32)
        xdt = xs_g * dt_x[:, sl]
        xdt_bf = xdt.astype(BF16)
        xdtw_bf = (xdt * dte_x[:, sl]).astype(BF16)
        b_g = b_ref[:, g * D_STATE:(g + 1) * D_STATE]
        c_g = c_ref[:, g * D_STATE:(g + 1) * D_STATE]
        cb = lax.dot_general(c_g, b_g, (((1,), (1,)), ((), ())), preferred_element_type=F32)
        y_g = jnp.zeros((L, gw), F32)
        for r in range(heads_per_group):
            hc = col0 + g * heads_per_group + r
            seg = acs[:, hc:hc + 1] - acs_t[hc:hc + 1, :]
            decay = jnp.exp(jnp.where(keep, seg, -jnp.inf))
            m = (cb * decay).astype(BF16)
            x_r = jnp.where(lane_head == r, xdt_bf, jnp.zeros_like(xdt_bf))
            y_g = y_g + jnp.dot(m, x_r, preferred_element_type=F32)
        st = st_ref[g]
        y_off = jnp.dot(c_g, st.astype(BF16), preferred_element_type=F32) * ea_x[:, sl]
        y_ref[:, sl] = (y_g + y_off).astype(y_ref.dtype)
        b_t = b_g.astype(F32).T.astype(BF16)
        st_ref[g] = st * cdec_x[:, sl] + jnp.dot(b_t, xdtw_bf, preferred_element_type=F32)


def _ssd_kernel(xs_f, b_f, c_f, dt_f, xs_b, b_b, c_b, dt_b, bias_ref, alog_ref, expand_ref,
                yf_ref, yb_ref, stf_ref, stb_ref, *, n_heads):
    @pl.when(pl.program_id(1) == 0)
    def _():
        stf_ref[...] = jnp.zeros_like(stf_ref)
        stb_ref[...] = jnp.zeros_like(stb_ref)

    _ssd_direction(xs_f, b_f, c_f, dt_f, bias_ref, alog_ref, expand_ref.at[0], stf_ref, yf_ref,
                   reverse=False, col0=0, n_heads=n_heads)
    _ssd_direction(xs_b, b_b, c_b, dt_b, bias_ref, alog_ref, expand_ref.at[1], stb_ref, yb_ref,
                   reverse=True, col0=n_heads, n_heads=n_heads)


def ssd_scan(xbc, dt_raw, dt_bias, a_log, *, batch, seq, d_ssd):
    t = batch * seq
    nc = seq // CHUNK
    n_heads = d_ssd // SSD_HEAD_DIM
    gn = SSD_GROUPS * D_STATE
    assert d_ssd % gn == 0 and 2 * n_heads <= LANE
    xb = d_ssd // gn
    bias = jnp.zeros((1, LANE), F32).at[0, :2 * n_heads].set(dt_bias.reshape(-1).astype(F32))
    alog = jnp.zeros((1, LANE), F32).at[0, :2 * n_heads].set(a_log.reshape(-1).astype(F32))
    lane_h = jnp.arange(d_ssd) // SSD_HEAD_DIM
    expand = jnp.stack([
        (jnp.arange(LANE)[:, None] == (d * n_heads + lane_h)[None, :]) for d in range(2)
    ]).astype(BF16)

    def fwd(b, c):
        return b * nc + c

    def bwd(b, c):
        return b * nc + (nc - 1 - c)

    def specs(rmap):
        return [
            pl.BlockSpec((CHUNK, d_ssd), lambda b, c: (rmap(b, c), 0)),
            pl.BlockSpec((CHUNK, gn), lambda b, c: (rmap(b, c), xb)),
            pl.BlockSpec((CHUNK, gn), lambda b, c: (rmap(b, c), xb + 1)),
            pl.BlockSpec((CHUNK, LANE), lambda b, c: (rmap(b, c), 0)),
        ]

    const2 = lambda b, c: (0, 0)
    gw = d_ssd // SSD_GROUPS
    return pl.pallas_call(
        functools.partial(_ssd_kernel, n_heads=n_heads),
        out_shape=(jax.ShapeDtypeStruct((t, d_ssd), BF16), jax.ShapeDtypeStruct((t, d_ssd), BF16)),
        grid=(batch, nc),
        in_specs=specs(fwd) + specs(bwd) + [
            pl.BlockSpec((1, LANE), const2),
            pl.BlockSpec((1, LANE), const2),
            pl.BlockSpec((2, LANE, d_ssd), lambda b, c: (0, 0, 0)),
        ],
        out_specs=(
            pl.BlockSpec((CHUNK, d_ssd), lambda b, c: (fwd(b, c), 0)),
            pl.BlockSpec((CHUNK, d_ssd), lambda b, c: (bwd(b, c), 0)),
        ),
        scratch_shapes=[pltpu.VMEM((SSD_GROUPS, D_STATE, gw), F32),
                        pltpu.VMEM((SSD_GROUPS, D_STATE, gw), F32)],
        compiler_params=_cparams(("parallel", "arbitrary")),
        name="ssd_scan",
    )(xbc, xbc, xbc, dt_raw, xbc, xbc, xbc, dt_raw, bias, alog, expand)


def _gelu(x):
    return 0.5 * x * (1.0 + lax.erf(x * (2.0 ** -0.5)))


def _mixer_out_kernel(z_ref, xs_ref, yf_ref, yb_ref, u_ref, v_ref, dskip_ref, sg_ref, gg_ref,
                      ws_ref, bs_ref, o_ref, *, d_ssd):
    gw = d_ssd // SSD_GROUPS
    y = (yf_ref[...].astype(F32) + yb_ref[...].astype(F32)
         + xs_ref[...].astype(F32) * dskip_ref[...]) * _silu(z_ref[...].astype(F32))
    for g in range(SSD_GROUPS):
        sl = slice(g * gw, (g + 1) * gw)
        o_ref[:, sl] = _rms_rows(y[:, sl], sg_ref[:, sl]).astype(o_ref.dtype)

    uu = _gelu(u_ref[...].astype(F32))
    vn = _rms_rows(_gelu(v_ref[...].astype(F32)), gg_ref[...]).astype(BF16)
    bs = bs_ref[...]
    for g in range(ws_ref.shape[0]):
        sl = slice(g * GMLP_GROUP_WIDTH, (g + 1) * GMLP_GROUP_WIDTH)
        sp = jnp.dot(ws_ref[g], vn[:, sl], preferred_element_type=F32) + bs[:, g:g + 1]
        o_ref[:, d_ssd + g * GMLP_GROUP_WIDTH:d_ssd + (g + 1) * GMLP_GROUP_WIDTH] = (
            uu[:, sl] * sp).astype(o_ref.dtype)


def mixer_out(proj, xbc, y_f, y_b, d_skip, ssd_norm_g, gmlp_norm_g, gmlp_ws, gmlp_bs, *, d_ssd, d_gmlp,
              conv_ch):
    t = proj.shape[0]
    assert d_ssd == d_gmlp and conv_ch % d_ssd == 0
    w = d_ssd
    u_blk = (d_ssd + conv_ch) // w
    n_groups = gmlp_ws.shape[0]
    dskip = jnp.repeat(d_skip.astype(F32), SSD_HEAD_DIM).reshape(1, d_ssd)
    row = lambda i: (i, 0)
    const = lambda i: (0, 0)
    return pl.pallas_call(
        functools.partial(_mixer_out_kernel, d_ssd=d_ssd),
        out_shape=jax.ShapeDtypeStruct((t, d_ssd + d_gmlp), BF16),
        grid=(t // CHUNK,),
        in_specs=[
            pl.BlockSpec((CHUNK, w), row),
            pl.BlockSpec((CHUNK, w), row),
            pl.BlockSpec((CHUNK, w), row),
            pl.BlockSpec((CHUNK, w), row),
            pl.BlockSpec((CHUNK, w), lambda i: (i, u_blk)),
            pl.BlockSpec((CHUNK, w), lambda i: (i, u_blk + 1)),
            pl.BlockSpec((1, w), const),
            pl.BlockSpec((1, w), const),
            pl.BlockSpec((1, w), const),
            pl.BlockSpec((n_groups, CHUNK, CHUNK), lambda i: (0, 0, 0)),
            pl.BlockSpec((CHUNK, n_groups), const),
        ],
        out_specs=pl.BlockSpec((CHUNK, d_ssd + d_gmlp), row),
        compiler_params=_cparams(("parallel",)),
        name="mixer_out",
    )(proj, xbc, y_f, y_b, proj, proj, dskip, ssd_norm_g.reshape(1, -1).astype(F32),
      gmlp_norm_g.reshape(1, -1).astype(F32), gmlp_ws.astype(BF16), gmlp_bs.T.astype(F32))


def _xattn_kernel(q_ref, k_ref, v_ref, o_ref, *, scale):
    s = lax.dot_general(q_ref[...], k_ref[...], (((1,), (1,)), ((), ())),
                        preferred_element_type=F32) * scale
    p = jnp.exp(s - jnp.max(s, axis=-1, keepdims=True))
    denom = jnp.sum(p, axis=-1, keepdims=True)
    o = jnp.dot(p.astype(BF16), v_ref[...], preferred_element_type=F32)
    o_ref[...] = (o / denom).astype(o_ref.dtype)


def cross_attention(q, kv, *, batch, seq, mem_len, tq=512):
    t, d = q.shape
    hd = d // XATTN_HEADS
    nq = seq // tq
    return pl.pallas_call(
        functools.partial(_xattn_kernel, scale=hd ** -0.5),
        out_shape=jax.ShapeDtypeStruct((t, d), BF16),
        grid=(batch, XATTN_HEADS, nq),
        in_specs=[
            pl.BlockSpec((tq, hd), lambda b, h, i: (b * nq + i, h)),
            pl.BlockSpec((mem_len, hd), lambda b, h, i: (b, h)),
            pl.BlockSpec((mem_len, hd), lambda b, h, i: (b, XATTN_HEADS + h)),
        ],
        out_specs=pl.BlockSpec((tq, hd), lambda b, h, i: (b * nq + i, h)),
        compiler_params=_cparams(("parallel", "parallel", "parallel")),
        name="cross_attention",
    )(q, kv, kv)


def _router_kernel(x_ref, g_ref, w_ref, aff_ref, *, n_experts):
    h = _rms_rows(x_ref[...], g_ref[...])
    logits = jnp.dot(h, w_ref[...], preferred_element_type=F32, precision=lax.Precision.HIGHEST)
    lt = logits.T[0:n_experts, :]
    p = jnp.exp(lt - jnp.max(lt, axis=0, keepdims=True))
    aff_ref[...] = p / jnp.sum(p, axis=0, keepdims=True)


def router_affinity(x, g, w_router, *, batch, seq, tm=256):
    t, d = x.shape
    e = w_router.shape[1]
    w_pad = jnp.zeros((d, LANE), F32).at[:, :e].set(w_router.astype(F32))
    ns = seq // tm
    return pl.pallas_call(
        functools.partial(_router_kernel, n_experts=e),
        out_shape=jax.ShapeDtypeStruct((batch, e, seq), F32),
        grid=(batch, ns),
        in_specs=[
            pl.BlockSpec((tm, d), lambda b, i: (b * ns + i, 0)),
            pl.BlockSpec((1, d), lambda b, i: (0, 0)),
            pl.BlockSpec((d, LANE), lambda b, i: (0, 0)),
        ],
        out_specs=pl.BlockSpec((None, e, tm), lambda b, i: (b, 0, i)),
        compiler_params=_cparams(("parallel", "parallel")),
        name="router",
    )(x, g.reshape(1, d).astype(F32), w_pad)


def _expert_up_kernel(idx_ref, x_hbm, g_ref, wg_ref, wu_ref, o_ref, stage_ref, xn_ref, sem, *, rows_half):
    e = pl.program_id(0)
    rows = xn_ref.shape[0]

    @pl.when(pl.program_id(1) == 0)
    def _():
        for half in range(rows // rows_half):
            base = half * rows_half

            def start(j, c):
                src = idx_ref[e * rows + base + j]
                pltpu.make_async_copy(x_hbm.at[pl.ds(src, 1)], stage_ref.at[pl.ds(j, 1)], sem).start()
                return c

            lax.fori_loop(0, rows_half, start, 0)

            def wait(j, c):
                pltpu.make_async_copy(x_hbm.at[pl.ds(0, 1)], stage_ref.at[pl.ds(j, 1)], sem).wait()
                return c

            lax.fori_loop(0, rows_half, wait, 0)

            def norm(i, c):
                r = pl.multiple_of(i * 32, 32)
                xn_ref[pl.ds(base + r, 32), :] = _rms_rows(
                    stage_ref[pl.ds(r, 32), :], g_ref[...]).astype(xn_ref.dtype)
                return c

            lax.fori_loop(0, rows_half // 32, norm, 0)

    xn = xn_ref[...]
    gate = jnp.dot(xn, wg_ref[...], preferred_element_type=F32)
    up = jnp.dot(xn, wu_ref[...], preferred_element_type=F32)
    o_ref[...] = (_silu(gate) * up).astype(o_ref.dtype)


def expert_up(idx_flat, x, g, w_gate_up, *, rows, d_expert, tf=512):
    t, d = x.shape
    e = w_gate_up.shape[0]
    nf = d_expert // tf
    rows_half = rows // 2
    grid_spec = pltpu.PrefetchScalarGridSpec(
        num_scalar_prefetch=1,
        grid=(e, nf),
        in_specs=[
            pl.BlockSpec(memory_space=pl.ANY),
            pl.BlockSpec((1, d), lambda i, f, idx: (0, 0)),
            pl.BlockSpec((None, d, tf), lambda i, f, idx: (i, 0, f)),
            pl.BlockSpec((None, d, tf), lambda i, f, idx: (i, 0, nf + f)),
        ],
        out_specs=pl.BlockSpec((None, rows, tf), lambda i, f, idx: (i, 0, f)),
        scratch_shapes=[
            pltpu.VMEM((rows_half, d), F32),
            pltpu.VMEM((rows, d), BF16),
            pltpu.SemaphoreType.DMA(()),
        ],
    )
    return pl.pallas_call(
        functools.partial(_expert_up_kernel, rows_half=rows_half),
        out_shape=jax.ShapeDtypeStruct((e, rows, d_expert), BF16),
        grid_spec=grid_spec,
        compiler_params=_cparams(("arbitrary", "arbitrary")),
        name="expert_up",
    )(idx_flat, x, g.reshape(1, d).astype(F32), w_gate_up, w_gate_up)


def _expert_down_kernel(a_ref, w_ref, gate_ref, o_ref):
    o_ref[...] = jnp.dot(a_ref[...], w_ref[...], preferred_element_type=F32) * gate_ref[...]


def expert_down(act, w_down, gates, *, tn=1024):
    e, rows, f = act.shape
    d = w_down.shape[2]
    return pl.pallas_call(
        _expert_down_kernel,
        out_shape=jax.ShapeDtypeStruct((e, rows, d), F32),
        grid=(e, d // tn),
        in_specs=[
            pl.BlockSpec((None, rows, f), lambda i, j: (i, 0, 0)),
            pl.BlockSpec((None, f, tn), lambda i, j: (i, 0, j)),
            pl.BlockSpec((None, rows, 1), lambda i, j: (i, 0, 0)),
        ],
        out_specs=pl.BlockSpec((None, rows, tn), lambda i, j: (i, 0, j)),
        compiler_params=_cparams(("parallel", "arbitrary")),
        name="expert_down",
    )(act, w_down, gates)


def _final_norm_kernel(x_ref, g_ref, o_ref):
    o_ref[...] = _rms_rows(x_ref[...], g_ref[...])


def final_norm(x, g, *, tm=256):
    t, d = x.shape
    return pl.pallas_call(
        _final_norm_kernel,
        out_shape=jax.ShapeDtypeStruct((t, d), F32),
        grid=(t // tm,),
        in_specs=[pl.BlockSpec((tm, d), lambda i: (i, 0)), pl.BlockSpec((1, d), lambda i: (0, 0))],
        out_specs=pl.BlockSpec((tm, d), lambda i: (i, 0)),
        compiler_params=_cparams(("parallel",)),
        name="final_norm",
    )(x, g.reshape(1, d).astype(F32))


def kernel(x, mem, norm_mix_g, w_in, conv_w, conv_b, dt_bias, a_log, d_skip, ssd_norm_g, gmlp_norm_g,
           gmlp_ws, gmlp_bs, w_out, norm_xattn_g, norm_mem_g, w_q, w_kv, w_o, norm_moe_g, w_router,
           w_gate_up, w_down, final_norm_g):
    batch, seq, d_model = x.shape
    mem_len = mem.shape[1]
    depth = w_in.shape[0]
    n_heads = dt_bias.shape[2]
    d_ssd = n_heads * SSD_HEAD_DIM
    conv_ch = conv_w.shape[2]
    d_gmlp = gmlp_norm_g.shape[1]
    d_expert = w_down.shape[2]
    n_experts = w_router.shape[2]
    cap = CAPACITY_FACTOR * seq // n_experts
    t = batch * seq
    o_dt = d_ssd + conv_ch
    o_u = o_dt + 2 * n_heads

    xf = x.reshape(t, d_model)
    memf = mem.reshape(batch * mem_len, d_model)
    for l in range(depth):
        w_main = jnp.concatenate([w_in[l][:, :o_dt], w_in[l][:, o_u:]], axis=1).astype(BF16)
        w_dt = jnp.zeros((d_model, LANE), BF16).at[:, :2 * n_heads].set(w_in[l][:, o_dt:o_u].astype(BF16))
        proj, dt_raw = norm_matmul(xf, norm_mix_g[l], w_main, w_dt)
        xbc = conv_silu(proj, conv_w[l], conv_b[l], batch=batch, seq=seq, col0=d_ssd)
        y_f, y_b = ssd_scan(xbc, dt_raw, dt_bias[l], a_log[l], batch=batch, seq=seq, d_ssd=d_ssd)
        y_cat = mixer_out(proj, xbc, y_f, y_b, d_skip[l], ssd_norm_g[l], gmlp_norm_g[l], gmlp_ws[l],
                          gmlp_bs[l], d_ssd=d_ssd, d_gmlp=d_gmlp, conv_ch=conv_ch)
        xf = matmul_residual(y_cat, w_out[l].astype(BF16), xf)

        q = norm_matmul(xf, norm_xattn_g[l], w_q[l].astype(BF16))
        kv = norm_matmul(memf, norm_mem_g[l], w_kv[l].astype(BF16))
        o = cross_attention(q, kv, batch=batch, seq=seq, mem_len=mem_len)
        xf = matmul_residual(o, w_o[l].astype(BF16), xf)

        aff = router_affinity(xf, norm_moe_g[l], w_router[l], batch=batch, seq=seq)
        gate, idx = lax.top_k(aff, cap)
        idx_flat = (idx + (jnp.arange(batch, dtype=jnp.int32) * seq)[:, None, None])
        idx_flat = jnp.swapaxes(idx_flat, 0, 1).reshape(-1)
        gates = jnp.swapaxes(gate, 0, 1).reshape(n_experts, batch * cap, 1)
        act = expert_up(idx_flat, xf, norm_moe_g[l], w_gate_up[l].astype(BF16), rows=batch * cap,
                        d_expert=d_expert)
        y = expert_down(act, w_down[l].astype(BF16), gates)
        xf = xf.at[idx_flat].add(y.reshape(-1, d_model))
    out = final_norm(xf, final_norm_g)
    return out.reshape(batch, seq, d_model)
```

```python
import functools

import jax
import jax.numpy as jnp
from jax import lax
from jax.experimental import pallas as pl
from jax.experimental.pallas import tpu as pltpu

F32 = jnp.float32
BF16 = jnp.bfloat16

RMS_EPS = 1e-6
SSD_HEAD_DIM = 64
SSD_GROUPS = 8
D_STATE = 128
D_CONV = 5
CHUNK = 128
GMLP_GROUP_WIDTH = 128
XATTN_HEADS = 4
N_EXPERTS = 16
CAPACITY_FACTOR = 2
LANE = 128
VMEM_LIMIT = 56 * 1024 * 1024


def _cparams(sem, vmem=VMEM_LIMIT):
    return pltpu.CompilerParams(dimension_semantics=sem, vmem_limit_bytes=vmem)


def _silu(x):
    return x * (1.0 / (1.0 + jnp.exp(-x)))


def _rms_rows(x, g):
    ms = jnp.mean(x * x, axis=-1, keepdims=True)
    return x * lax.rsqrt(ms + RMS_EPS) * g


def _norm_cast_kernel(x_ref, g_ref, o_ref):
    o_ref[...] = _rms_rows(x_ref[...], g_ref[...]).astype(o_ref.dtype)


def norm_cast(x, g, *, tm=256):
    m, k = x.shape
    tm = min(tm, m)
    return pl.pallas_call(
        _norm_cast_kernel,
        out_shape=jax.ShapeDtypeStruct((m, k), BF16),
        grid=(m // tm,),
        in_specs=[pl.BlockSpec((tm, k), lambda i: (i, 0)), pl.BlockSpec((1, k), lambda i: (0, 0))],
        out_specs=pl.BlockSpec((tm, k), lambda i: (i, 0)),
        compiler_params=_cparams(("parallel",)),
        name="norm_cast",
    )(x, g.reshape(1, k).astype(F32))


def _mm_kernel(a_ref, w_ref, o_ref):
    o_ref[...] = jnp.dot(a_ref[...], w_ref[...].astype(BF16),
                         preferred_element_type=F32).astype(o_ref.dtype)


def _mm_res_kernel(a_ref, w_ref, r_ref, o_ref):
    o_ref[...] = r_ref[...] + jnp.dot(a_ref[...], w_ref[...].astype(BF16), preferred_element_type=F32)


def matmul(a, w, layer, *, n_cols=None, res=None, out_dtype=BF16, tm=1024, tn=512):
    m, k = a.shape
    n = w.shape[2] if n_cols is None else n_cols
    tm = min(tm, m)
    tn = min(tn, n)
    assert m % tm == 0 and n % tn == 0
    a_spec = pl.BlockSpec((tm, k), lambda i, j: (i, 0))
    w_spec = pl.BlockSpec((None, k, tn), lambda i, j: (layer, 0, j))
    o_spec = pl.BlockSpec((tm, tn), lambda i, j: (i, j))
    if res is None:
        return pl.pallas_call(
            _mm_kernel,
            out_shape=jax.ShapeDtypeStruct((m, n), out_dtype),
            grid=(m // tm, n // tn),
            in_specs=[a_spec, w_spec],
            out_specs=o_spec,
            compiler_params=_cparams(("parallel", "arbitrary")),
            name="matmul",
        )(a, w)
    return pl.pallas_call(
        _mm_res_kernel,
        out_shape=jax.ShapeDtypeStruct((m, n), F32),
        grid=(m // tm, n // tn),
        in_specs=[a_spec, w_spec, o_spec],
        out_specs=o_spec,
        compiler_params=_cparams(("parallel", "arbitrary")),
        name="matmul_residual",
    )(a, w, res)


HALO = 16


def _conv_kernel(prev_ref, cur_ref, next_ref, w_ref, b_ref, o_ref, ext_ref, *, n_seq_tiles):
    ts = cur_ref.shape[0]
    s = pl.program_id(1)
    prev = prev_ref[...].astype(F32)
    nxt = next_ref[...].astype(F32)
    ext_ref[0:HALO, :] = jnp.where(s == 0, 0.0, prev)
    ext_ref[HALO:HALO + ts, :] = cur_ref[...].astype(F32)
    ext_ref[HALO + ts:, :] = jnp.where(s == n_seq_tiles - 1, 0.0, nxt)
    acc = jnp.zeros(cur_ref.shape, F32) + b_ref[...]
    for k in range(D_CONV):
        off = HALO - D_CONV // 2 + k
        acc = acc + ext_ref[off:off + ts, :] * w_ref[k:k + 1, :]
    o_ref[...] = _silu(acc).astype(o_ref.dtype)


def conv_silu(proj, conv_w, conv_b, *, batch, seq, col0, ts=512, tc=1024):
    c = conv_w.shape[1]
    t = batch * seq
    nst = seq // ts
    hb = ts // HALO
    cb0 = col0 // tc
    assert col0 % tc == 0 and c % tc == 0 and seq % ts == 0
    last_halo = t // HALO - 1

    def prev_map(b, s, j):
        return (jnp.maximum((b * nst + s) * hb - 1, 0), cb0 + j)

    def next_map(b, s, j):
        return (jnp.minimum((b * nst + s + 1) * hb, last_halo), cb0 + j)

    return pl.pallas_call(
        functools.partial(_conv_kernel, n_seq_tiles=nst),
        out_shape=jax.ShapeDtypeStruct((t, c), BF16),
        grid=(batch, nst, c // tc),
        in_specs=[
            pl.BlockSpec((HALO, tc), prev_map),
            pl.BlockSpec((ts, tc), lambda b, s, j: (b * nst + s, cb0 + j)),
            pl.BlockSpec((HALO, tc), next_map),
            pl.BlockSpec((D_CONV, tc), lambda b, s, j: (0, j)),
            pl.BlockSpec((1, tc), lambda b, s, j: (0, j)),
        ],
        out_specs=pl.BlockSpec((ts, tc), lambda b, s, j: (b * nst + s, j)),
        scratch_shapes=[pltpu.VMEM((ts + 2 * HALO, tc), F32)],
        compiler_params=_cparams(("parallel", "parallel", "parallel")),
        name="conv_silu",
    )(proj, proj, proj, conv_w.astype(F32), conv_b.reshape(1, c).astype(F32))


def _split_hi_lo(x):
    hi = x.astype(BF16)
    lo = (x - hi.astype(F32)).astype(BF16)
    return hi, lo


def _ssd_direction(xs_ref, b_ref, c_ref, dt_ref, bias_ref, alog_ref, expand_ref, st_ref, y_ref,
                   *, reverse, col0, n_heads):
    L = CHUNK
    hp = SSD_HEAD_DIM
    heads_per_group = n_heads // SSD_GROUPS
    gw = heads_per_group * hp
    row = lax.broadcasted_iota(jnp.int32, (L, L), 0)
    col = lax.broadcasted_iota(jnp.int32, (L, L), 1)
    keep = (col >= row) if reverse else (col <= row)
    tri = jnp.where(keep, 1.0, 0.0).astype(F32)

    xraw = dt_ref[...] + bias_ref[...]
    dt = jnp.maximum(xraw, 0.0) + jnp.log1p(jnp.exp(-jnp.abs(xraw)))
    a = -jnp.exp(alog_ref[...])
    dta = dt * a
    acs = jnp.dot(tri, dta, preferred_element_type=F32, precision=lax.Precision.HIGHEST)
    acs_t = acs.T
    total = acs[0:1, :] if reverse else acs[L - 1:L, :]
    dte = jnp.exp(total - acs)
    ea = jnp.exp(acs)
    cdec = jnp.broadcast_to(jnp.exp(total), (8, LANE))

    stack = jnp.concatenate([dt, dte, ea, cdec], axis=0)
    s_hi, s_lo = _split_hi_lo(stack)
    e = expand_ref[...]
    ex = (jnp.dot(s_hi, e, preferred_element_type=F32)
          + jnp.dot(s_lo, e, preferred_element_type=F32))
    dt_x = ex[0:L]
    dte_x = ex[L:2 * L]
    ea_x = ex[2 * L:3 * L]
    cdec_x = ex[3 * L:3 * L + 1]

    lane_head = lax.broadcasted_iota(jnp.int32, (L, gw), 1) // hp
    for g in range(SSD_GROUPS):
        sl = slice(g * gw, (g + 1) * gw)
        xs_g = xs_ref[:, sl].astype(F32)
        xdt = xs_g * dt_x[:, sl]
        xdt_bf = xdt.astype(BF16)
        xdtw_bf = (xdt * dte_x[:, sl]).astype(BF16)
        b_g = b_ref[:, g * D_STATE:(g + 1) * D_STATE]
        c_g = c_ref[:, g * D_STATE:(g + 1) * D_STATE]
        cb = lax.dot_general(c_g, b_g, (((1,), (1,)), ((), ())), preferred_element_type=F32)
        y_g = jnp.zeros((L, gw), F32)
        for r in range(heads_per_group):
            hc = col0 + g * heads_per_group + r
            seg = acs[:, hc:hc + 1] - acs_t[hc:hc + 1, :]
            decay = jnp.exp(jnp.where(keep, seg, -jnp.inf))
            m = (cb * decay).astype(BF16)
            x_r = jnp.where(lane_head == r, xdt_bf, jnp.zeros_like(xdt_bf))
            y_g = y_g + jnp.dot(m, x_r, preferred_element_type=F32)
        st = st_ref[g]
        y_off = jnp.dot(c_g, st.astype(BF16), preferred_element_type=F32) * ea_x[:, sl]
        y_ref[:, sl] = (y_g + y_off).astype(y_ref.dtype)
        b_t = b_g.astype(F32).T.astype(BF16)
        st_ref[g] = st * cdec_x[:, sl] + jnp.dot(b_t, xdtw_bf, preferred_element_type=F32)


def _ssd_kernel(xs_f, b_f, c_f, dt_f, xs_b, b_b, c_b, dt_b, bias_ref, alog_ref, expand_ref,
                yf_ref, yb_ref, stf_ref, stb_ref, *, n_heads):
    @pl.when(pl.program_id(1) == 0)
    def _():
        stf_ref[...] = jnp.zeros_like(stf_ref)
        stb_ref[...] = jnp.zeros_like(stb_ref)

    _ssd_direction(xs_f, b_f, c_f, dt_f, bias_ref, alog_ref, expand_ref.at[0], stf_ref, yf_ref,
                   reverse=False, col0=0, n_heads=n_heads)
    _ssd_direction(xs_b, b_b, c_b, dt_b, bias_ref, alog_ref, expand_ref.at[1], stb_ref, yb_ref,
                   reverse=True, col0=n_heads, n_heads=n_heads)


def ssd_scan(xbc, dt_raw, dt_bias, a_log, *, batch, seq, d_ssd):
    t = batch * seq
    nc = seq // CHUNK
    n_heads = d_ssd // SSD_HEAD_DIM
    gn = SSD_GROUPS * D_STATE
    assert d_ssd % gn == 0 and 2 * n_heads <= LANE
    xb = d_ssd // gn
    bias = jnp.zeros((1, LANE), F32).at[0, :2 * n_heads].set(dt_bias.reshape(-1).astype(F32))
    alog = jnp.zeros((1, LANE), F32).at[0, :2 * n_heads].set(a_log.reshape(-1).astype(F32))
    lane_h = jnp.arange(d_ssd) // SSD_HEAD_DIM
    expand = jnp.stack([
        (jnp.arange(LANE)[:, None] == (d * n_heads + lane_h)[None, :]) for d in range(2)
    ]).astype(BF16)

    def fwd(b, c):
        return b * nc + c

    def bwd(b, c):
        return b * nc + (nc - 1 - c)

    def specs(rmap):
        return [
            pl.BlockSpec((CHUNK, d_ssd), lambda b, c: (rmap(b, c), 0)),
            pl.BlockSpec((CHUNK, gn), lambda b, c: (rmap(b, c), xb)),
            pl.BlockSpec((CHUNK, gn), lambda b, c: (rmap(b, c), xb + 1)),
            pl.BlockSpec((CHUNK, LANE), lambda b, c: (rmap(b, c), 0)),
        ]

    const2 = lambda b, c: (0, 0)
    gw = d_ssd // SSD_GROUPS
    return pl.pallas_call(
        functools.partial(_ssd_kernel, n_heads=n_heads),
        out_shape=(jax.ShapeDtypeStruct((t, d_ssd), BF16), jax.ShapeDtypeStruct((t, d_ssd), BF16)),
        grid=(batch, nc),
        in_specs=specs(fwd) + specs(bwd) + [
            pl.BlockSpec((1, LANE), const2),
            pl.BlockSpec((1, LANE), const2),
            pl.BlockSpec((2, LANE, d_ssd), lambda b, c: (0, 0, 0)),
        ],
        out_specs=(
            pl.BlockSpec((CHUNK, d_ssd), lambda b, c: (fwd(b, c), 0)),
            pl.BlockSpec((CHUNK, d_ssd), lambda b, c: (bwd(b, c), 0)),
        ),
        scratch_shapes=[pltpu.VMEM((SSD_GROUPS, D_STATE, gw), F32),
                        pltpu.VMEM((SSD_GROUPS, D_STATE, gw), F32)],
        compiler_params=_cparams(("parallel", "arbitrary")),
        name="ssd_scan",
    )(xbc, xbc, xbc, dt_raw, xbc, xbc, xbc, dt_raw, bias, alog, expand)


def _gelu(x):
    return 0.5 * x * (1.0 + lax.erf(x * (2.0 ** -0.5)))


def _mixer_out_kernel(z_ref, xs_ref, yf_ref, yb_ref, u_ref, v_ref, dskip_ref, sg_ref, gg_ref,
                      ws_ref, bs_ref, o_ref, *, d_ssd):
    gw = d_ssd // SSD_GROUPS
    y = (yf_ref[...].astype(F32) + yb_ref[...].astype(F32)
         + xs_ref[...].astype(F32) * dskip_ref[...]) * _silu(z_ref[...].astype(F32))
    for g in range(SSD_GROUPS):
        sl = slice(g * gw, (g + 1) * gw)
        o_ref[:, sl] = _rms_rows(y[:, sl], sg_ref[:, sl]).astype(o_ref.dtype)

    uu = _gelu(u_ref[...].astype(F32))
    vn = _rms_rows(_gelu(v_ref[...].astype(F32)), gg_ref[...]).astype(BF16)
    bs = bs_ref[...]
    for g in range(ws_ref.shape[0]):
        sl = slice(g * GMLP_GROUP_WIDTH, (g + 1) * GMLP_GROUP_WIDTH)
        sp = jnp.dot(ws_ref[g], vn[:, sl], preferred_element_type=F32) + bs[:, g:g + 1]
        o_ref[:, d_ssd + g * GMLP_GROUP_WIDTH:d_ssd + (g + 1) * GMLP_GROUP_WIDTH] = (
            uu[:, sl] * sp).astype(o_ref.dtype)


def mixer_out(proj, proj_uv, xbc, y_f, y_b, d_skip, ssd_norm_g, gmlp_norm_g, gmlp_ws, gmlp_bs, *, d_ssd,
              d_gmlp):
    t = proj.shape[0]
    assert d_ssd == d_gmlp
    w = d_ssd
    n_groups = gmlp_ws.shape[0]
    dskip = jnp.repeat(d_skip.astype(F32), SSD_HEAD_DIM).reshape(1, d_ssd)
    row = lambda i: (i, 0)
    const = lambda i: (0, 0)
    return pl.pallas_call(
        functools.partial(_mixer_out_kernel, d_ssd=d_ssd),
        out_shape=jax.ShapeDtypeStruct((t, d_ssd + d_gmlp), BF16),
        grid=(t // CHUNK,),
        in_specs=[
            pl.BlockSpec((CHUNK, w), row),
            pl.BlockSpec((CHUNK, w), row),
            pl.BlockSpec((CHUNK, w), row),
            pl.BlockSpec((CHUNK, w), row),
            pl.BlockSpec((CHUNK, w), row),
            pl.BlockSpec((CHUNK, w), lambda i: (i, 1)),
            pl.BlockSpec((1, w), const),
            pl.BlockSpec((1, w), const),
            pl.BlockSpec((1, w), const),
            pl.BlockSpec((n_groups, CHUNK, CHUNK), lambda i: (0, 0, 0)),
            pl.BlockSpec((CHUNK, n_groups), const),
        ],
        out_specs=pl.BlockSpec((CHUNK, d_ssd + d_gmlp), row),
        compiler_params=_cparams(("parallel",)),
        name="mixer_out",
    )(proj, xbc, y_f, y_b, proj_uv, proj_uv, dskip, ssd_norm_g.reshape(1, -1).astype(F32),
      gmlp_norm_g.reshape(1, -1).astype(F32), gmlp_ws.astype(BF16), gmlp_bs.T.astype(F32))


def _xattn_kernel(q_ref, k_ref, v_ref, o_ref, *, scale):
    s = lax.dot_general(q_ref[...], k_ref[...], (((1,), (1,)), ((), ())),
                        preferred_element_type=F32) * scale
    p = jnp.exp(s - jnp.max(s, axis=-1, keepdims=True))
    denom = jnp.sum(p, axis=-1, keepdims=True)
    o = jnp.dot(p.astype(BF16), v_ref[...], preferred_element_type=F32)
    o_ref[...] = (o / denom).astype(o_ref.dtype)


def cross_attention(q, kv, *, batch, seq, mem_len, tq=512):
    t, d = q.shape
    hd = d // XATTN_HEADS
    nq = seq // tq
    return pl.pallas_call(
        functools.partial(_xattn_kernel, scale=hd ** -0.5),
        out_shape=jax.ShapeDtypeStruct((t, d), BF16),
        grid=(batch, XATTN_HEADS, nq),
        in_specs=[
            pl.BlockSpec((tq, hd), lambda b, h, i: (b * nq + i, h)),
            pl.BlockSpec((mem_len, hd), lambda b, h, i: (b, h)),
            pl.BlockSpec((mem_len, hd), lambda b, h, i: (b, XATTN_HEADS + h)),
        ],
        out_specs=pl.BlockSpec((tq, hd), lambda b, h, i: (b * nq + i, h)),
        compiler_params=_cparams(("parallel", "parallel", "parallel")),
        name="cross_attention",
    )(q, kv, kv)


def _router_kernel(x_ref, g_ref, w_ref, afft_ref, affn_ref, *, n_experts):
    h = _rms_rows(x_ref[...], g_ref[...])
    logits = jnp.dot(h, w_ref[...], preferred_element_type=F32, precision=lax.Precision.HIGHEST)
    lane = lax.broadcasted_iota(jnp.int32, logits.shape, 1)
    valid = lane < n_experts
    lm = jnp.where(valid, logits, -jnp.inf)
    q = jnp.where(valid, jnp.exp(lm - jnp.max(lm, axis=1, keepdims=True)), 0.0)
    aff = q / jnp.sum(q, axis=1, keepdims=True)
    affn_ref[...] = aff
    afft_ref[...] = aff.T[0:n_experts, :]


def router_affinity(x, g, w_router, layer, *, batch, seq, tm=256):
    t, d = x.shape
    e = w_router.shape[2]
    w_pad = jnp.zeros((d, LANE), F32).at[:, :e].set(w_router[layer].astype(F32))
    ns = seq // tm
    return pl.pallas_call(
        functools.partial(_router_kernel, n_experts=e),
        out_shape=(jax.ShapeDtypeStruct((batch, e, seq), F32), jax.ShapeDtypeStruct((t, LANE), F32)),
        grid=(batch, ns),
        in_specs=[
            pl.BlockSpec((tm, d), lambda b, i: (b * ns + i, 0)),
            pl.BlockSpec((1, d), lambda b, i: (0, 0)),
            pl.BlockSpec((d, LANE), lambda b, i: (0, 0)),
        ],
        out_specs=(pl.BlockSpec((None, e, tm), lambda b, i: (b, 0, i)),
                   pl.BlockSpec((tm, LANE), lambda b, i: (b * ns + i, 0))),
        compiler_params=_cparams(("parallel", "parallel")),
        name="router",
    )(x, g.reshape(1, d).astype(F32), w_pad)


BISECT_ITERS = 152


def _route_kernel(afft_ref, affn_ref, idx_ref, gate_ref, slot_ref, *, cap):
    n_exp, seq = afft_ref.shape
    blk = LANE
    nblk = seq // blk
    capf = float(cap)

    xt = afft_ref[...]

    def bisect(_, carry):
        lo, hi = carry
        mid = 0.5 * (lo + hi)
        cnt = jnp.sum(jnp.where(xt >= mid, 1.0, 0.0), axis=1, keepdims=True)
        ge = cnt >= capf
        return jnp.where(ge, mid, lo), jnp.where(ge, hi, mid)

    lo, hi = lax.fori_loop(0, BISECT_ITERS, bisect,
                           (jnp.zeros((n_exp, 1), F32), jnp.full((n_exp, 1), 2.0, F32)))
    r_i = lax.broadcasted_iota(jnp.int32, (n_exp, LANE), 0)
    c_i = lax.broadcasted_iota(jnp.int32, (n_exp, LANE), 1)
    diag = r_i == c_i
    lane_ok = lax.broadcasted_iota(jnp.int32, (1, LANE), 1) < n_exp
    lo_r = jnp.where(lane_ok, jnp.sum(jnp.where(diag, lo, 0.0), axis=0, keepdims=True), 4.0)
    hi_r = jnp.where(lane_ok, jnp.sum(jnp.where(diag, hi, 0.0), axis=0, keepdims=True), 4.0)

    n_gt = jnp.sum(jnp.where(affn_ref[...] >= hi_r, 1.0, 0.0), axis=0, keepdims=True)
    need = capf - n_gt

    tr = lax.broadcasted_iota(jnp.int32, (blk, blk), 0)
    tc = lax.broadcasted_iota(jnp.int32, (blk, blk), 1)
    tril = jnp.where(tr >= tc, 1.0, 0.0).astype(BF16)
    carry_eq = jnp.zeros((1, LANE), F32)
    carry_sel = jnp.zeros((1, LANE), F32)
    for k in range(nblk):
        x = affn_ref[k * blk:(k + 1) * blk, :]
        gt = x >= hi_r
        eq = jnp.logical_and(x >= lo_r, jnp.logical_not(gt))
        pos_eq = jnp.dot(tril, jnp.where(eq, 1.0, 0.0).astype(BF16), preferred_element_type=F32) + carry_eq
        carry_eq = pos_eq[blk - 1:blk, :]
        sel = jnp.logical_or(gt, jnp.logical_and(eq, pos_eq <= need))
        cs = jnp.dot(tril, jnp.where(sel, 1.0, 0.0).astype(BF16), preferred_element_type=F32) + carry_sel
        carry_sel = cs[blk - 1:blk, :]
        slot_ref[k * blk:(k + 1) * blk, :] = jnp.where(sel, cs - 1.0, -1.0)

    s_iota = lax.broadcasted_iota(jnp.int32, (blk, cap), 1).astype(F32)
    t_iota = lax.broadcasted_iota(jnp.int32, (blk, 1), 0).astype(F32)
    for e in range(n_exp):
        def body(k, acc):
            acc_i, acc_g = acc
            r0 = pl.multiple_of(k * blk, blk)
            slot = slot_ref[pl.ds(r0, blk), :][:, e:e + 1]
            aff = affn_ref[pl.ds(r0, blk), :][:, e:e + 1]
            hit = jnp.broadcast_to(slot, (blk, cap)) == s_iota
            tok = t_iota + lax.convert_element_type(k * blk, F32)
            acc_i = acc_i + jnp.sum(jnp.where(hit, tok, 0.0), axis=0, keepdims=True)
            acc_g = acc_g + jnp.sum(jnp.where(hit, aff, 0.0), axis=0, keepdims=True)
            return acc_i, acc_g

        acc_i, acc_g = lax.fori_loop(0, nblk, body, (jnp.zeros((1, cap), F32), jnp.zeros((1, cap), F32)))
        idx_ref[e:e + 1, :] = acc_i.astype(jnp.int32)
        gate_ref[e:e + 1, :] = acc_g


def route(aff_t, aff_n, *, cap):
    batch, e, seq = aff_t.shape
    return pl.pallas_call(
        functools.partial(_route_kernel, cap=cap),
        out_shape=(jax.ShapeDtypeStruct((batch, e, cap), jnp.int32),
                   jax.ShapeDtypeStruct((batch, e, cap), F32)),
        grid=(batch,),
        in_specs=[pl.BlockSpec((None, e, seq), lambda b: (b, 0, 0)),
                  pl.BlockSpec((seq, LANE), lambda b: (b, 0))],
        out_specs=(pl.BlockSpec((None, e, cap), lambda b: (b, 0, 0)),
                   pl.BlockSpec((None, e, cap), lambda b: (b, 0, 0))),
        scratch_shapes=[pltpu.VMEM((seq, LANE), F32)],
        compiler_params=_cparams(("parallel",)),
        name="route",
    )(aff_t, aff_n)


def _row_copy(src_ref, src_row, dst_ref, dst_row, sem):
    return pltpu.make_async_copy(src_ref.at[pl.ds(src_row, 1)], dst_ref.at[pl.ds(dst_row, 1)], sem)


def _expert_up_kernel(idx_ref, x_hbm, g_ref, wg_ref, wu_ref, o_ref, stage_ref, xn_ref, sem):
    e = pl.program_id(0)
    n_exp = pl.num_programs(0)
    rows = xn_ref.shape[0]

    def start_gather(expert):
        def body(j, c):
            for p in range(2):
                r = 2 * j + p
                _row_copy(x_hbm, idx_ref[expert * rows + r], stage_ref, r, sem).start(priority=p)
            return c

        lax.fori_loop(0, rows // 2, body, 0)

    @pl.when(pl.program_id(1) == 0)
    def _():
        @pl.when(e == 0)
        def _():
            start_gather(0)

        def wait(j, c):
            _row_copy(x_hbm, 0, stage_ref, j, sem).wait()
            return c

        lax.fori_loop(0, rows, wait, 0)

        def norm(i, c):
            r = pl.multiple_of(i * 32, 32)
            xn_ref[pl.ds(r, 32), :] = _rms_rows(stage_ref[pl.ds(r, 32), :], g_ref[...]).astype(xn_ref.dtype)
            return c

        lax.fori_loop(0, rows // 32, norm, 0)

        @pl.when(e + 1 < n_exp)
        def _():
            start_gather(e + 1)

    xn = xn_ref[...]
    gate = jnp.dot(xn, wg_ref[...].astype(BF16), preferred_element_type=F32)
    up = jnp.dot(xn, wu_ref[...].astype(BF16), preferred_element_type=F32)
    o_ref[...] = (_silu(gate) * up).astype(o_ref.dtype)


def expert_up(idx_flat, x, g, w_gate_up, layer, *, rows, d_expert, tf=256):
    t, d = x.shape
    e = w_gate_up.shape[1]
    nf = d_expert // tf
    grid_spec = pltpu.PrefetchScalarGridSpec(
        num_scalar_prefetch=1,
        grid=(e, nf),
        in_specs=[
            pl.BlockSpec(memory_space=pl.ANY),
            pl.BlockSpec((1, d), lambda i, f, idx: (0, 0)),
            pl.BlockSpec((None, None, d, tf), lambda i, f, idx: (layer, i, 0, f)),
            pl.BlockSpec((None, None, d, tf), lambda i, f, idx: (layer, i, 0, nf + f)),
        ],
        out_specs=pl.BlockSpec((None, rows, tf), lambda i, f, idx: (i, 0, f)),
        scratch_shapes=[
            pltpu.VMEM((rows, d), F32),
            pltpu.VMEM((rows, d), BF16),
            pltpu.SemaphoreType.DMA(()),
        ],
    )
    return pl.pallas_call(
        _expert_up_kernel,
        out_shape=jax.ShapeDtypeStruct((e, rows, d_expert), BF16),
        grid_spec=grid_spec,
        compiler_params=_cparams(("arbitrary", "arbitrary")),
        name="expert_up",
    )(idx_flat, x, g.reshape(1, d).astype(F32), w_gate_up, w_gate_up)


def _expert_down_kernel(idx_ref, act_ref, w_ref, gate_ref, x_hbm, o_hbm, y_ref, rows_ref, gsem, ssem):
    del x_hbm
    e = pl.program_id(0)
    j = pl.program_id(1)
    n_exp = pl.num_programs(0)
    nn = pl.num_programs(1)
    rows = y_ref.shape[0]
    tn = w_ref.shape[1]

    def wait_scatter():
        def body(r, c):
            _row_copy(rows_ref, r, o_hbm, 0, ssem).wait()
            return c

        lax.fori_loop(0, rows, body, 0)

    @pl.when(j == nn // 2)
    def _():
        @pl.when(e > 0)
        def _():
            wait_scatter()

        def body(r2, c):
            for p in range(2):
                r = 2 * r2 + p
                _row_copy(o_hbm, idx_ref[e * rows + r], rows_ref, r, gsem).start(priority=p)
            return c

        lax.fori_loop(0, rows // 2, body, 0)

    col = pl.multiple_of(j * tn, tn)
    y_ref[:, pl.ds(col, tn)] = jnp.dot(act_ref[...], w_ref[...].astype(BF16),
                                       preferred_element_type=F32) * gate_ref[...]

    @pl.when(j == nn - 1)
    def _():
        def wait(r, c):
            _row_copy(o_hbm, 0, rows_ref, r, gsem).wait()
            return c

        lax.fori_loop(0, rows, wait, 0)

        def add(i, c):
            r = pl.multiple_of(i * 32, 32)
            rows_ref[pl.ds(r, 32), :] = rows_ref[pl.ds(r, 32), :] + y_ref[pl.ds(r, 32), :]
            return c

        lax.fori_loop(0, rows // 32, add, 0)

        def body(r2, c):
            for p in range(2):
                r = 2 * r2 + p
                _row_copy(rows_ref, r, o_hbm, idx_ref[e * rows + r], ssem).start(priority=p)
            return c

        lax.fori_loop(0, rows // 2, body, 0)

        @pl.when(e == n_exp - 1)
        def _():
            wait_scatter()


def expert_down_scatter(idx_flat, act, w_down, gates, x, layer, *, tn=512):
    e, rows, f = act.shape
    t, d = x.shape
    grid_spec = pltpu.PrefetchScalarGridSpec(
        num_scalar_prefetch=1,
        grid=(e, d // tn),
        in_specs=[
            pl.BlockSpec((None, rows, f), lambda i, j, idx: (i, 0, 0)),
            pl.BlockSpec((None, None, f, tn), lambda i, j, idx: (layer, i, 0, j)),
            pl.BlockSpec((None, rows, 1), lambda i, j, idx: (i, 0, 0)),
            pl.BlockSpec(memory_space=pl.ANY),
        ],
        out_specs=pl.BlockSpec(memory_space=pl.ANY),
        scratch_shapes=[
            pltpu.VMEM((rows, d), F32),
            pltpu.VMEM((rows, d), F32),
            pltpu.SemaphoreType.DMA(()),
            pltpu.SemaphoreType.DMA(()),
        ],
    )
    return pl.pallas_call(
        _expert_down_kernel,
        out_shape=jax.ShapeDtypeStruct((t, d), F32),
        grid_spec=grid_spec,
        input_output_aliases={4: 0},
        compiler_params=_cparams(("arbitrary", "arbitrary")),
        name="expert_down_scatter",
    )(idx_flat, act, w_down, gates, x)


def _final_norm_kernel(x_ref, g_ref, o_ref):
    o_ref[...] = _rms_rows(x_ref[...], g_ref[...])


def final_norm(x, g, *, tm=256):
    t, d = x.shape
    return pl.pallas_call(
        _final_norm_kernel,
        out_shape=jax.ShapeDtypeStruct((t, d), F32),
        grid=(t // tm,),
        in_specs=[pl.BlockSpec((tm, d), lambda i: (i, 0)), pl.BlockSpec((1, d), lambda i: (0, 0))],
        out_specs=pl.BlockSpec((tm, d), lambda i: (i, 0)),
        compiler_params=_cparams(("parallel",)),
        name="final_norm",
    )(x, g.reshape(1, d).astype(F32))


def kernel(x, mem, norm_mix_g, w_in, conv_w, conv_b, dt_bias, a_log, d_skip, ssd_norm_g, gmlp_norm_g,
           gmlp_ws, gmlp_bs, w_out, norm_xattn_g, norm_mem_g, w_q, w_kv, w_o, norm_moe_g, w_router,
           w_gate_up, w_down, final_norm_g):
    batch, seq, d_model = x.shape
    mem_len = mem.shape[1]
    depth = w_in.shape[0]
    n_heads = dt_bias.shape[2]
    d_ssd = n_heads * SSD_HEAD_DIM
    conv_ch = conv_w.shape[2]
    d_gmlp = gmlp_norm_g.shape[1]
    d_expert = w_down.shape[2]
    n_experts = w_router.shape[2]
    cap = CAPACITY_FACTOR * seq // n_experts
    t = batch * seq
    o_dt = d_ssd + conv_ch
    o_u = o_dt + 2 * n_heads

    xf = x.reshape(t, d_model)
    memf = mem.reshape(batch * mem_len, d_model)
    w_uv = w_in[:, :, o_u:]
    w_dt = jnp.zeros((depth, d_model, LANE), F32).at[:, :, :2 * n_heads].set(w_in[:, :, o_dt:o_u])
    for l in range(depth):
        h = norm_cast(xf, norm_mix_g[l])
        proj = matmul(h, w_in, l, n_cols=o_dt)
        proj_uv = matmul(h, w_uv, l)
        dt_raw = matmul(h, w_dt, l, out_dtype=F32)
        xbc = conv_silu(proj, conv_w[l], conv_b[l], batch=batch, seq=seq, col0=d_ssd)
        y_f, y_b = ssd_scan(xbc, dt_raw, dt_bias[l], a_log[l], batch=batch, seq=seq, d_ssd=d_ssd)
        y_cat = mixer_out(proj, proj_uv, xbc, y_f, y_b, d_skip[l], ssd_norm_g[l], gmlp_norm_g[l],
                          gmlp_ws[l], gmlp_bs[l], d_ssd=d_ssd, d_gmlp=d_gmlp)
        xf = matmul(y_cat, w_out, l, res=xf)

        q = matmul(norm_cast(xf, norm_xattn_g[l]), w_q, l)
        kv = matmul(norm_cast(memf, norm_mem_g[l]), w_kv, l)
        o = cross_attention(q, kv, batch=batch, seq=seq, mem_len=mem_len)
        xf = matmul(o, w_o, l, res=xf)

        aff_t, aff_n = router_affinity(xf, norm_moe_g[l], w_router, l, batch=batch, seq=seq)
        idx, gate = route(aff_t, aff_n, cap=cap)
        idx_flat = idx + (jnp.arange(batch, dtype=jnp.int32) * seq)[:, None, None]
        idx_flat = jnp.swapaxes(idx_flat, 0, 1).reshape(-1)
        gates = jnp.swapaxes(gate, 0, 1).reshape(n_experts, batch * cap, 1)
        act = expert_up(idx_flat, xf, norm_moe_g[l], w_gate_up, l, rows=batch * cap, d_expert=d_expert)
        xf = expert_down_scatter(idx_flat, act, w_down, gates, xf, l)
    out = final_norm(xf, final_norm_g)
    return out.reshape(batch, seq, d_model)
```

```python
import functools

import jax
import jax.numpy as jnp
from jax import lax
from jax.experimental import pallas as pl
from jax.experimental.pallas import tpu as pltpu

F32 = jnp.float32
BF16 = jnp.bfloat16

RMS_EPS = 1e-6
SSD_HEAD_DIM = 64
SSD_GROUPS = 8
D_STATE = 128
D_CONV = 5
CHUNK = 128
GMLP_GROUP_WIDTH = 128
XATTN_HEADS = 4
N_EXPERTS = 16
CAPACITY_FACTOR = 2
LANE = 128
VMEM_LIMIT = 56 * 1024 * 1024


def _cparams(sem, vmem=VMEM_LIMIT):
    return pltpu.CompilerParams(dimension_semantics=sem, vmem_limit_bytes=vmem)


def _silu(x):
    return x * (1.0 / (1.0 + jnp.exp(-x)))


def _rms_rows(x, g):
    ms = jnp.mean(x * x, axis=-1, keepdims=True)
    return x * lax.rsqrt(ms + RMS_EPS) * g


def _norm_cast_kernel(x_ref, g_ref, o_ref):
    o_ref[...] = _rms_rows(x_ref[...], g_ref[...]).astype(o_ref.dtype)


def norm_cast(x, g, *, tm=256):
    m, k = x.shape
    tm = min(tm, m)
    return pl.pallas_call(
        _norm_cast_kernel,
        out_shape=jax.ShapeDtypeStruct((m, k), BF16),
        grid=(m // tm,),
        in_specs=[pl.BlockSpec((tm, k), lambda i: (i, 0)), pl.BlockSpec((1, k), lambda i: (0, 0))],
        out_specs=pl.BlockSpec((tm, k), lambda i: (i, 0)),
        compiler_params=_cparams(("parallel",)),
        name="norm_cast",
    )(x, g.reshape(1, k).astype(F32))


def _mm_kernel(a_ref, w_ref, o_ref):
    o_ref[...] = jnp.dot(a_ref[...], w_ref[...].astype(BF16),
                         preferred_element_type=F32).astype(o_ref.dtype)


def _mm_nt_kernel(a_ref, wt_ref, o_ref):
    o_ref[...] = lax.dot_general(a_ref[...], wt_ref[...].astype(BF16), (((1,), (1,)), ((), ())),
                                 preferred_element_type=F32).astype(o_ref.dtype)


def matmul_nt(a, wt, row0, n, *, out_dtype=BF16, tm=1024, tn=512):
    m, k = a.shape
    tm = min(tm, m)
    tn = min(tn, n)
    assert m % tm == 0 and n % tn == 0 and row0 % 8 == 0 and wt.shape[1] == k
    return pl.pallas_call(
        _mm_nt_kernel,
        out_shape=jax.ShapeDtypeStruct((m, n), out_dtype),
        grid=(m // tm, n // tn),
        in_specs=[pl.BlockSpec((tm, k), lambda i, j: (i, 0)),
                  pl.BlockSpec((pl.Element(tn), pl.Element(k)),
                               lambda i, j: (pl.multiple_of(row0 + j * tn, 8), 0))],
        out_specs=pl.BlockSpec((tm, tn), lambda i, j: (i, j)),
        compiler_params=_cparams(("parallel", "arbitrary")),
        name="matmul_nt",
    )(a, wt)


def _mm_res_kernel(a_ref, w_ref, r_ref, o_ref):
    o_ref[...] = r_ref[...] + jnp.dot(a_ref[...], w_ref[...].astype(BF16), preferred_element_type=F32)


def matmul(a, w, layer, *, n_cols=None, res=None, out_dtype=BF16, tm=1024, tn=512):
    m, k = a.shape
    n = w.shape[2] if n_cols is None else n_cols
    tm = min(tm, m)
    tn = min(tn, n)
    assert m % tm == 0 and n % tn == 0
    a_spec = pl.BlockSpec((tm, k), lambda i, j: (i, 0))
    w_spec = pl.BlockSpec((None, k, tn), lambda i, j: (layer, 0, j))
    o_spec = pl.BlockSpec((tm, tn), lambda i, j: (i, j))
    if res is None:
        return pl.pallas_call(
            _mm_kernel,
            out_shape=jax.ShapeDtypeStruct((m, n), out_dtype),
            grid=(m // tm, n // tn),
            in_specs=[a_spec, w_spec],
            out_specs=o_spec,
            compiler_params=_cparams(("parallel", "arbitrary")),
            name="matmul",
        )(a, w)
    return pl.pallas_call(
        _mm_res_kernel,
        out_shape=jax.ShapeDtypeStruct((m, n), F32),
        grid=(m // tm, n // tn),
        in_specs=[a_spec, w_spec, o_spec],
        out_specs=o_spec,
        compiler_params=_cparams(("parallel", "arbitrary")),
        name="matmul_residual",
    )(a, w, res)


HALO = 16


def _conv_kernel(prev_ref, cur_ref, next_ref, w_ref, b_ref, o_ref, ext_ref, *, n_seq_tiles):
    ts = cur_ref.shape[0]
    s = pl.program_id(1)
    prev = prev_ref[...].astype(F32)
    nxt = next_ref[...].astype(F32)
    ext_ref[0:HALO, :] = jnp.where(s == 0, 0.0, prev)
    ext_ref[HALO:HALO + ts, :] = cur_ref[...].astype(F32)
    ext_ref[HALO + ts:, :] = jnp.where(s == n_seq_tiles - 1, 0.0, nxt)
    acc = jnp.zeros(cur_ref.shape, F32) + b_ref[...]
    for k in range(D_CONV):
        off = HALO - D_CONV // 2 + k
        acc = acc + ext_ref[off:off + ts, :] * w_ref[k:k + 1, :]
    o_ref[...] = _silu(acc).astype(o_ref.dtype)


def conv_silu(proj, conv_w, conv_b, *, batch, seq, col0, ts=512, tc=1024):
    c = conv_w.shape[1]
    t = batch * seq
    nst = seq // ts
    hb = ts // HALO
    cb0 = col0 // tc
    assert col0 % tc == 0 and c % tc == 0 and seq % ts == 0
    last_halo = t // HALO - 1

    def prev_map(b, s, j):
        return (jnp.maximum((b * nst + s) * hb - 1, 0), cb0 + j)

    def next_map(b, s, j):
        return (jnp.minimum((b * nst + s + 1) * hb, last_halo), cb0 + j)

    return pl.pallas_call(
        functools.partial(_conv_kernel, n_seq_tiles=nst),
        out_shape=jax.ShapeDtypeStruct((t, c), BF16),
        grid=(batch, nst, c // tc),
        in_specs=[
            pl.BlockSpec((HALO, tc), prev_map),
            pl.BlockSpec((ts, tc), lambda b, s, j: (b * nst + s, cb0 + j)),
            pl.BlockSpec((HALO, tc), next_map),
            pl.BlockSpec((D_CONV, tc), lambda b, s, j: (0, j)),
            pl.BlockSpec((1, tc), lambda b, s, j: (0, j)),
        ],
        out_specs=pl.BlockSpec((ts, tc), lambda b, s, j: (b * nst + s, j)),
        scratch_shapes=[pltpu.VMEM((ts + 2 * HALO, tc), F32)],
        compiler_params=_cparams(("parallel", "parallel", "parallel")),
        name="conv_silu",
    )(proj, proj, proj, conv_w.astype(F32), conv_b.reshape(1, c).astype(F32))


def _ssd_direction(xs_ref, b_ref, c_ref, dt_ref, bias_ref, alog_ref, expand_ref, st_ref, y_ref,
                   *, reverse, col0, n_heads):
    L = CHUNK
    hp = SSD_HEAD_DIM
    heads_per_group = n_heads // SSD_GROUPS
    gw = heads_per_group * hp
    row = lax.broadcasted_iota(jnp.int32, (L, L), 0)
    col = lax.broadcasted_iota(jnp.int32, (L, L), 1)
    keep = (col >= row) if reverse else (col <= row)
    tri = jnp.where(keep, 1.0, 0.0).astype(F32)

    xraw = dt_ref[...] + bias_ref[...]
    dt = jnp.maximum(xraw, 0.0) + jnp.log1p(jnp.exp(-jnp.abs(xraw)))
    a = -jnp.exp(alog_ref[...])
    dta = dt * a
    acs = jnp.dot(tri, dta, preferred_element_type=F32, precision=lax.Precision.HIGHEST)
    acs_t = acs.T
    total = acs[0:1, :] if reverse else acs[L - 1:L, :]
    dte = jnp.exp(total - acs)
    ea = jnp.exp(acs)
    cdec = jnp.broadcast_to(jnp.exp(total), (8, LANE))

    stack = jnp.concatenate([dt, dte, ea, cdec], axis=0).astype(BF16)
    ex = jnp.dot(stack, expand_ref[...], preferred_element_type=F32)
    dt_x = ex[0:L]
    dte_x = ex[L:2 * L]
    ea_x = ex[2 * L:3 * L]
    cdec_x = ex[3 * L:3 * L + 1]

    lane_head = lax.broadcasted_iota(jnp.int32, (L, gw), 1) // hp
    for g in range(SSD_GROUPS):
        sl = slice(g * gw, (g + 1) * gw)
        xs_g = xs_ref[:, sl].astype(F32)
        xdt = xs_g * dt_x[:, sl]
        xdt_bf = xdt.astype(BF16)
        xdtw_bf = (xdt * dte_x[:, sl]).astype(BF16)
        b_g = b_ref[:, g * D_STATE:(g + 1) * D_STATE]
        c_g = c_ref[:, g * D_STATE:(g + 1) * D_STATE]
        cb = lax.dot_general(c_g, b_g, (((1,), (1,)), ((), ())), preferred_element_type=F32)
        y_g = jnp.zeros((L, gw), F32)
        for r in range(heads_per_group):
            hc = col0 + g * heads_per_group + r
            seg = acs[:, hc:hc + 1] - acs_t[hc:hc + 1, :]
            decay = jnp.exp(jnp.where(keep, seg, -jnp.inf))
            m = (cb * decay).astype(BF16)
            x_r = jnp.where(lane_head == r, xdt_bf, jnp.zeros_like(xdt_bf))
            y_g = y_g + jnp.dot(m, x_r, preferred_element_type=F32)
        st = st_ref[g]
        y_off = jnp.dot(c_g, st.astype(BF16), preferred_element_type=F32) * ea_x[:, sl]
        y_ref[:, sl] = (y_g + y_off).astype(y_ref.dtype)
        b_t = b_g.astype(F32).T.astype(BF16)
        st_ref[g] = st * cdec_x[:, sl] + jnp.dot(b_t, xdtw_bf, preferred_element_type=F32)


def _ssd_kernel(xs_f, b_f, c_f, dt_f, xs_b, b_b, c_b, dt_b, bias_ref, alog_ref, expand_ref,
                yf_ref, yb_ref, stf_ref, stb_ref, *, n_heads):
    @pl.when(pl.program_id(1) == 0)
    def _():
        stf_ref[...] = jnp.zeros_like(stf_ref)
        stb_ref[...] = jnp.zeros_like(stb_ref)

    _ssd_direction(xs_f, b_f, c_f, dt_f, bias_ref, alog_ref, expand_ref.at[0], stf_ref, yf_ref,
                   reverse=False, col0=0, n_heads=n_heads)
    _ssd_direction(xs_b, b_b, c_b, dt_b, bias_ref, alog_ref, expand_ref.at[1], stb_ref, yb_ref,
                   reverse=True, col0=n_heads, n_heads=n_heads)


def ssd_scan(xbc, dt_raw, dt_bias, a_log, *, batch, seq, d_ssd):
    t = batch * seq
    nc = seq // CHUNK
    n_heads = d_ssd // SSD_HEAD_DIM
    gn = SSD_GROUPS * D_STATE
    assert d_ssd % gn == 0 and 2 * n_heads <= LANE
    xb = d_ssd // gn
    bias = jnp.zeros((1, LANE), F32).at[0, :2 * n_heads].set(dt_bias.reshape(-1).astype(F32))
    alog = jnp.zeros((1, LANE), F32).at[0, :2 * n_heads].set(a_log.reshape(-1).astype(F32))
    lane_h = jnp.arange(d_ssd) // SSD_HEAD_DIM
    expand = jnp.stack([
        (jnp.arange(LANE)[:, None] == (d * n_heads + lane_h)[None, :]) for d in range(2)
    ]).astype(BF16)

    def fwd(b, c):
        return b * nc + c

    def bwd(b, c):
        return b * nc + (nc - 1 - c)

    def specs(rmap):
        return [
            pl.BlockSpec((CHUNK, d_ssd), lambda b, c: (rmap(b, c), 0)),
            pl.BlockSpec((CHUNK, gn), lambda b, c: (rmap(b, c), xb)),
            pl.BlockSpec((CHUNK, gn), lambda b, c: (rmap(b, c), xb + 1)),
            pl.BlockSpec((CHUNK, LANE), lambda b, c: (rmap(b, c), 0)),
        ]

    const2 = lambda b, c: (0, 0)
    gw = d_ssd // SSD_GROUPS
    return pl.pallas_call(
        functools.partial(_ssd_kernel, n_heads=n_heads),
        out_shape=(jax.ShapeDtypeStruct((t, d_ssd), BF16), jax.ShapeDtypeStruct((t, d_ssd), BF16)),
        grid=(batch, nc),
        in_specs=specs(fwd) + specs(bwd) + [
            pl.BlockSpec((1, LANE), const2),
            pl.BlockSpec((1, LANE), const2),
            pl.BlockSpec((2, LANE, d_ssd), lambda b, c: (0, 0, 0)),
        ],
        out_specs=(
            pl.BlockSpec((CHUNK, d_ssd), lambda b, c: (fwd(b, c), 0)),
            pl.BlockSpec((CHUNK, d_ssd), lambda b, c: (bwd(b, c), 0)),
        ),
        scratch_shapes=[pltpu.VMEM((SSD_GROUPS, D_STATE, gw), F32),
                        pltpu.VMEM((SSD_GROUPS, D_STATE, gw), F32)],
        compiler_params=_cparams(("parallel", "arbitrary")),
        name="ssd_scan",
    )(xbc, xbc, xbc, dt_raw, xbc, xbc, xbc, dt_raw, bias, alog, expand)


def _gelu(x):
    return 0.5 * x * (1.0 + lax.erf(x * (2.0 ** -0.5)))


def _mixer_out_kernel(z_ref, xs_ref, yf_ref, yb_ref, u_ref, v_ref, dskip_ref, sg_ref, gg_ref,
                      ws_ref, bs_ref, o_ref, *, d_ssd):
    gw = d_ssd // SSD_GROUPS
    y = (yf_ref[...].astype(F32) + yb_ref[...].astype(F32)
         + xs_ref[...].astype(F32) * dskip_ref[...]) * _silu(z_ref[...].astype(F32))
    for g in range(SSD_GROUPS):
        sl = slice(g * gw, (g + 1) * gw)
        o_ref[:, sl] = _rms_rows(y[:, sl], sg_ref[:, sl]).astype(o_ref.dtype)

    uu = _gelu(u_ref[...].astype(F32))
    vn = _rms_rows(_gelu(v_ref[...].astype(F32)), gg_ref[...]).astype(BF16)
    bs = bs_ref[...]
    for g in range(ws_ref.shape[0]):
        sl = slice(g * GMLP_GROUP_WIDTH, (g + 1) * GMLP_GROUP_WIDTH)
        sp = jnp.dot(ws_ref[g], vn[:, sl], preferred_element_type=F32) + bs[:, g:g + 1]
        o_ref[:, d_ssd + g * GMLP_GROUP_WIDTH:d_ssd + (g + 1) * GMLP_GROUP_WIDTH] = (
            uu[:, sl] * sp).astype(o_ref.dtype)


def mixer_out(proj, proj_uv, xbc, y_f, y_b, d_skip, ssd_norm_g, gmlp_norm_g, gmlp_ws, gmlp_bs, *, d_ssd,
              d_gmlp):
    t = proj.shape[0]
    assert d_ssd == d_gmlp
    w = d_ssd
    n_groups = gmlp_ws.shape[0]
    dskip = jnp.repeat(d_skip.astype(F32), SSD_HEAD_DIM).reshape(1, d_ssd)
    row = lambda i: (i, 0)
    const = lambda i: (0, 0)
    return pl.pallas_call(
        functools.partial(_mixer_out_kernel, d_ssd=d_ssd),
        out_shape=jax.ShapeDtypeStruct((t, d_ssd + d_gmlp), BF16),
        grid=(t // CHUNK,),
        in_specs=[
            pl.BlockSpec((CHUNK, w), row),
            pl.BlockSpec((CHUNK, w), row),
            pl.BlockSpec((CHUNK, w), row),
            pl.BlockSpec((CHUNK, w), row),
            pl.BlockSpec((CHUNK, w), row),
            pl.BlockSpec((CHUNK, w), lambda i: (i, 1)),
            pl.BlockSpec((1, w), const),
            pl.BlockSpec((1, w), const),
            pl.BlockSpec((1, w), const),
            pl.BlockSpec((n_groups, CHUNK, CHUNK), lambda i: (0, 0, 0)),
            pl.BlockSpec((CHUNK, n_groups), const),
        ],
        out_specs=pl.BlockSpec((CHUNK, d_ssd + d_gmlp), row),
        compiler_params=_cparams(("parallel",)),
        name="mixer_out",
    )(proj, xbc, y_f, y_b, proj_uv, proj_uv, dskip, ssd_norm_g.reshape(1, -1).astype(F32),
      gmlp_norm_g.reshape(1, -1).astype(F32), gmlp_ws.astype(BF16), gmlp_bs.T.astype(F32))


def _xattn_kernel(q_ref, k_ref, v_ref, o_ref, *, scale):
    s = lax.dot_general(q_ref[...], k_ref[...], (((1,), (1,)), ((), ())),
                        preferred_element_type=F32) * scale
    p = jnp.exp(s - jnp.max(s, axis=-1, keepdims=True))
    denom = jnp.sum(p, axis=-1, keepdims=True)
    o = jnp.dot(p.astype(BF16), v_ref[...], preferred_element_type=F32)
    o_ref[...] = (o / denom).astype(o_ref.dtype)


def cross_attention(q, kv, *, batch, seq, mem_len, tq=512):
    t, d = q.shape
    hd = d // XATTN_HEADS
    nq = seq // tq
    return pl.pallas_call(
        functools.partial(_xattn_kernel, scale=hd ** -0.5),
        out_shape=jax.ShapeDtypeStruct((t, d), BF16),
        grid=(batch, XATTN_HEADS, nq),
        in_specs=[
            pl.BlockSpec((tq, hd), lambda b, h, i: (b * nq + i, h)),
            pl.BlockSpec((mem_len, hd), lambda b, h, i: (b, h)),
            pl.BlockSpec((mem_len, hd), lambda b, h, i: (b, XATTN_HEADS + h)),
        ],
        out_specs=pl.BlockSpec((tq, hd), lambda b, h, i: (b * nq + i, h)),
        compiler_params=_cparams(("parallel", "parallel", "parallel")),
        name="cross_attention",
    )(q, kv, kv)


def _router_kernel(x_ref, g_ref, w_ref, afft_ref, affn_ref, *, n_experts):
    h = _rms_rows(x_ref[...], g_ref[...])
    logits = jnp.dot(h, w_ref[...], preferred_element_type=F32, precision=lax.Precision.HIGHEST)
    lane = lax.broadcasted_iota(jnp.int32, logits.shape, 1)
    valid = lane < n_experts
    lm = jnp.where(valid, logits, -jnp.inf)
    q = jnp.where(valid, jnp.exp(lm - jnp.max(lm, axis=1, keepdims=True)), 0.0)
    aff = q / jnp.sum(q, axis=1, keepdims=True)
    affn_ref[...] = aff
    afft_ref[...] = aff.T[0:n_experts, :]


def router_affinity(x, g, w_router, layer, *, batch, seq, tm=256):
    t, d = x.shape
    e = w_router.shape[2]
    w_pad = jnp.zeros((d, LANE), F32).at[:, :e].set(w_router[layer].astype(F32))
    ns = seq // tm
    return pl.pallas_call(
        functools.partial(_router_kernel, n_experts=e),
        out_shape=(jax.ShapeDtypeStruct((batch, e, seq), F32), jax.ShapeDtypeStruct((t, LANE), F32)),
        grid=(batch, ns),
        in_specs=[
            pl.BlockSpec((tm, d), lambda b, i: (b * ns + i, 0)),
            pl.BlockSpec((1, d), lambda b, i: (0, 0)),
            pl.BlockSpec((d, LANE), lambda b, i: (0, 0)),
        ],
        out_specs=(pl.BlockSpec((None, e, tm), lambda b, i: (b, 0, i)),
                   pl.BlockSpec((tm, LANE), lambda b, i: (b * ns + i, 0))),
        compiler_params=_cparams(("parallel", "parallel")),
        name="router",
    )(x, g.reshape(1, d).astype(F32), w_pad)


BISECT_ITERS = 152


def _route_kernel(afft_ref, affn_ref, idx_ref, gate_ref, slot_ref, *, cap):
    n_exp, seq = afft_ref.shape
    blk = LANE
    nblk = seq // blk
    capf = float(cap)

    xt = afft_ref[...]

    def bisect(_, carry):
        lo, hi = carry
        mid = 0.5 * (lo + hi)
        cnt = jnp.sum(jnp.where(xt >= mid, 1.0, 0.0), axis=1, keepdims=True)
        ge = cnt >= capf
        return jnp.where(ge, mid, lo), jnp.where(ge, hi, mid)

    lo, hi = lax.fori_loop(0, BISECT_ITERS, bisect,
                           (jnp.zeros((n_exp, 1), F32), jnp.full((n_exp, 1), 2.0, F32)))
    r_i = lax.broadcasted_iota(jnp.int32, (n_exp, LANE), 0)
    c_i = lax.broadcasted_iota(jnp.int32, (n_exp, LANE), 1)
    diag = r_i == c_i
    lane_ok = lax.broadcasted_iota(jnp.int32, (1, LANE), 1) < n_exp
    lo_r = jnp.where(lane_ok, jnp.sum(jnp.where(diag, lo, 0.0), axis=0, keepdims=True), 4.0)
    hi_r = jnp.where(lane_ok, jnp.sum(jnp.where(diag, hi, 0.0), axis=0, keepdims=True), 4.0)

    n_gt = jnp.sum(jnp.where(affn_ref[...] >= hi_r, 1.0, 0.0), axis=0, keepdims=True)
    need = capf - n_gt

    tr = lax.broadcasted_iota(jnp.int32, (blk, blk), 0)
    tc = lax.broadcasted_iota(jnp.int32, (blk, blk), 1)
    tril = jnp.where(tr >= tc, 1.0, 0.0).astype(BF16)
    carry_eq = jnp.zeros((1, LANE), F32)
    carry_sel = jnp.zeros((1, LANE), F32)
    for k in range(nblk):
        x = affn_ref[k * blk:(k + 1) * blk, :]
        gt = x >= hi_r
        eq = jnp.logical_and(x >= lo_r, jnp.logical_not(gt))
        pos_eq = jnp.dot(tril, jnp.where(eq, 1.0, 0.0).astype(BF16), preferred_element_type=F32) + carry_eq
        carry_eq = pos_eq[blk - 1:blk, :]
        sel = jnp.logical_or(gt, jnp.logical_and(eq, pos_eq <= need))
        cs = jnp.dot(tril, jnp.where(sel, 1.0, 0.0).astype(BF16), preferred_element_type=F32) + carry_sel
        carry_sel = cs[blk - 1:blk, :]
        slot_ref[k * blk:(k + 1) * blk, :] = jnp.where(sel, cs - 1.0, -1.0)

    s_iota = lax.broadcasted_iota(jnp.int32, (blk, cap), 1).astype(F32)
    t_iota = lax.broadcasted_iota(jnp.int32, (blk, 1), 0).astype(F32)
    for e in range(n_exp):
        def body(k, acc):
            acc_i, acc_g = acc
            r0 = pl.multiple_of(k * blk, blk)
            slot = slot_ref[pl.ds(r0, blk), :][:, e:e + 1]
            aff = affn_ref[pl.ds(r0, blk), :][:, e:e + 1]
            hit = jnp.broadcast_to(slot, (blk, cap)) == s_iota
            tok = t_iota + lax.convert_element_type(k * blk, F32)
            acc_i = acc_i + jnp.sum(jnp.where(hit, tok, 0.0), axis=0, keepdims=True)
            acc_g = acc_g + jnp.sum(jnp.where(hit, aff, 0.0), axis=0, keepdims=True)
            return acc_i, acc_g

        acc_i, acc_g = lax.fori_loop(0, nblk, body, (jnp.zeros((1, cap), F32), jnp.zeros((1, cap), F32)))
        idx_ref[e:e + 1, :] = acc_i.astype(jnp.int32)
        gate_ref[e:e + 1, :] = acc_g


def route(aff_t, aff_n, *, cap):
    batch, e, seq = aff_t.shape
    return pl.pallas_call(
        functools.partial(_route_kernel, cap=cap),
        out_shape=(jax.ShapeDtypeStruct((batch, e, cap), jnp.int32),
                   jax.ShapeDtypeStruct((batch, e, cap), F32)),
        grid=(batch,),
        in_specs=[pl.BlockSpec((None, e, seq), lambda b: (b, 0, 0)),
                  pl.BlockSpec((seq, LANE), lambda b: (b, 0))],
        out_specs=(pl.BlockSpec((None, e, cap), lambda b: (b, 0, 0)),
                   pl.BlockSpec((None, e, cap), lambda b: (b, 0, 0))),
        scratch_shapes=[pltpu.VMEM((seq, LANE), F32)],
        compiler_params=_cparams(("parallel",)),
        name="route",
    )(aff_t, aff_n)


def _row_copy(src_ref, src_row, dst_ref, dst_row, sem):
    return pltpu.make_async_copy(src_ref.at[pl.ds(src_row, 1)], dst_ref.at[pl.ds(dst_row, 1)], sem)


ROW_UNROLL = 8


def _expert_up_kernel(idx_ref, x_hbm, g_ref, wg_ref, wu_ref, o_ref, stage_ref, xn_ref, sem):
    e = pl.program_id(0)
    n_exp = pl.num_programs(0)
    rows = xn_ref.shape[0]

    def start_gather(expert):
        def body(j, c):
            for p in range(ROW_UNROLL):
                r = ROW_UNROLL * j + p
                _row_copy(x_hbm, idx_ref[expert * rows + r], stage_ref, r, sem).start(priority=p % 2)
            return c

        lax.fori_loop(0, rows // ROW_UNROLL, body, 0)

    @pl.when(pl.program_id(1) == 0)
    def _():
        @pl.when(e == 0)
        def _():
            start_gather(0)

        pltpu.make_async_copy(x_hbm.at[pl.ds(0, rows)], stage_ref, sem).wait()

        def norm(i, c):
            r = pl.multiple_of(i * 32, 32)
            xn_ref[pl.ds(r, 32), :] = _rms_rows(stage_ref[pl.ds(r, 32), :], g_ref[...]).astype(xn_ref.dtype)
            return c

        lax.fori_loop(0, rows // 32, norm, 0)

        @pl.when(e + 1 < n_exp)
        def _():
            start_gather(e + 1)

    xn = xn_ref[...]
    gate = jnp.dot(xn, wg_ref[...].astype(BF16), preferred_element_type=F32)
    up = jnp.dot(xn, wu_ref[...].astype(BF16), preferred_element_type=F32)
    o_ref[...] = (_silu(gate) * up).astype(o_ref.dtype)


def expert_up(idx_flat, x, g, w_gate_up, layer, *, rows, d_expert, tf=256):
    t, d = x.shape
    e = w_gate_up.shape[1]
    nf = d_expert // tf
    grid_spec = pltpu.PrefetchScalarGridSpec(
        num_scalar_prefetch=1,
        grid=(e, nf),
        in_specs=[
            pl.BlockSpec(memory_space=pl.ANY),
            pl.BlockSpec((1, d), lambda i, f, idx: (0, 0)),
            pl.BlockSpec((None, None, d, tf), lambda i, f, idx: (layer, i, 0, f)),
            pl.BlockSpec((None, None, d, tf), lambda i, f, idx: (layer, i, 0, nf + f)),
        ],
        out_specs=pl.BlockSpec((None, rows, tf), lambda i, f, idx: (i, 0, f)),
        scratch_shapes=[
            pltpu.VMEM((rows, d), F32),
            pltpu.VMEM((rows, d), BF16),
            pltpu.SemaphoreType.DMA(()),
        ],
    )
    return pl.pallas_call(
        _expert_up_kernel,
        out_shape=jax.ShapeDtypeStruct((e, rows, d_expert), BF16),
        grid_spec=grid_spec,
        compiler_params=_cparams(("arbitrary", "arbitrary")),
        name="expert_up",
    )(idx_flat, x, g.reshape(1, d).astype(F32), w_gate_up, w_gate_up)


def _expert_down_kernel(act_ref, w_ref, gate_ref, o_ref):
    o_ref[...] = (jnp.dot(act_ref[...], w_ref[...].astype(BF16), preferred_element_type=F32)
                  * gate_ref[...]).astype(o_ref.dtype)


def expert_down(act, w_down, gates, layer, *, tn=512):
    e, rows, f = act.shape
    d = w_down.shape[3]
    return pl.pallas_call(
        _expert_down_kernel,
        out_shape=jax.ShapeDtypeStruct((e, rows, d), BF16),
        grid=(e, d // tn),
        in_specs=[
            pl.BlockSpec((None, rows, f), lambda i, j: (i, 0, 0)),
            pl.BlockSpec((None, None, f, tn), lambda i, j: (layer, i, 0, j)),
            pl.BlockSpec((None, rows, 1), lambda i, j: (i, 0, 0)),
        ],
        out_specs=pl.BlockSpec((None, rows, tn), lambda i, j: (i, 0, j)),
        compiler_params=_cparams(("parallel", "arbitrary")),
        name="expert_down",
    )(act, w_down, gates)


SUBLANE = 8
COMBINE_CHUNK = 512


def _combine_kernel(idx_ref, y_ref, x_hbm, o_hbm, acc_ref, y3_ref, stage_ref, sem_in, sem_out,
                    *, seq, cap, n_batch):
    n = pl.program_id(0)
    b = pl.program_id(1)
    e = pl.program_id(2)
    n_exp = pl.num_programs(2)
    cn = y_ref.shape[1]
    ch = stage_ref.shape[1]
    n_chunks = seq // ch
    col = pl.multiple_of(n * cn, cn)

    def x_copy(c):
        row = pl.multiple_of(b * seq + c * ch, ch)
        return pltpu.make_async_copy(x_hbm.at[pl.ds(row, ch), pl.ds(col, cn)], stage_ref.at[c % 2],
                                     sem_in.at[c % 2])

    def o_copy(c):
        row = pl.multiple_of(b * seq + c * ch, ch)
        return pltpu.make_async_copy(stage_ref.at[c % 2], o_hbm.at[pl.ds(row, ch), pl.ds(col, cn)],
                                     sem_out.at[c % 2])

    @pl.when(e == 0)
    def _():
        x_copy(0).start()
        for c in range(n_chunks):
            x_copy(c).wait()
            if c + 1 < n_chunks:
                x_copy(c + 1).start()
            acc_ref[c * ch:(c + 1) * ch] = stage_ref[c % 2].reshape(ch, SUBLANE, cn // SUBLANE)

    y3_ref[...] = y_ref[...].astype(F32).reshape(cap, SUBLANE, cn // SUBLANE)
    base = (e * n_batch + b) * cap

    def body(i, c):
        toks = [idx_ref[base + ROW_UNROLL * i + u] for u in range(ROW_UNROLL)]
        vals = [acc_ref[toks[u]] + y3_ref[ROW_UNROLL * i + u] for u in range(ROW_UNROLL)]
        for u in range(ROW_UNROLL):
            acc_ref[toks[u]] = vals[u]
        return c

    lax.fori_loop(0, cap // ROW_UNROLL, body, 0)

    @pl.when(e == n_exp - 1)
    def _():
        for c in range(n_chunks):
            if c >= 2:
                o_copy(c - 2).wait()
            stage_ref[c % 2] = acc_ref[c * ch:(c + 1) * ch].reshape(ch, cn)
            o_copy(c).start()
        for c in range(max(n_chunks - 2, 0), n_chunks):
            o_copy(c).wait()


def moe_combine(idx_local, y, x, *, batch, seq, cap, cn=1024):
    e, rows, d = y.shape
    t = x.shape[0]
    grid_spec = pltpu.PrefetchScalarGridSpec(
        num_scalar_prefetch=1,
        grid=(d // cn, batch, e),
        in_specs=[
            pl.BlockSpec((None, cap, cn), lambda n, b, i, idx: (i, b, n)),
            pl.BlockSpec(memory_space=pl.ANY),
        ],
        out_specs=pl.BlockSpec(memory_space=pl.ANY),
        scratch_shapes=[
            pltpu.VMEM((seq, SUBLANE, cn // SUBLANE), F32),
            pltpu.VMEM((cap, SUBLANE, cn // SUBLANE), F32),
            pltpu.VMEM((2, min(COMBINE_CHUNK, seq), cn), F32),
            pltpu.SemaphoreType.DMA((2,)),
            pltpu.SemaphoreType.DMA((2,)),
        ],
    )
    return pl.pallas_call(
        functools.partial(_combine_kernel, seq=seq, cap=cap, n_batch=batch),
        out_shape=jax.ShapeDtypeStruct((t, d), F32),
        grid_spec=grid_spec,
        compiler_params=_cparams(("arbitrary", "arbitrary", "arbitrary")),
        name="moe_combine",
    )(idx_local, y, x)


def _final_norm_kernel(x_ref, g_ref, o_ref):
    o_ref[...] = _rms_rows(x_ref[...], g_ref[...])


def final_norm(x, g, *, tm=256):
    t, d = x.shape
    return pl.pallas_call(
        _final_norm_kernel,
        out_shape=jax.ShapeDtypeStruct((t, d), F32),
        grid=(t // tm,),
        in_specs=[pl.BlockSpec((tm, d), lambda i: (i, 0)), pl.BlockSpec((1, d), lambda i: (0, 0))],
        out_specs=pl.BlockSpec((tm, d), lambda i: (i, 0)),
        compiler_params=_cparams(("parallel",)),
        name="final_norm",
    )(x, g.reshape(1, d).astype(F32))


def kernel(x, mem, norm_mix_g, w_in, conv_w, conv_b, dt_bias, a_log, d_skip, ssd_norm_g, gmlp_norm_g,
           gmlp_ws, gmlp_bs, w_out, norm_xattn_g, norm_mem_g, w_q, w_kv, w_o, norm_moe_g, w_router,
           w_gate_up, w_down, final_norm_g):
    batch, seq, d_model = x.shape
    mem_len = mem.shape[1]
    depth = w_in.shape[0]
    n_heads = dt_bias.shape[2]
    d_ssd = n_heads * SSD_HEAD_DIM
    conv_ch = conv_w.shape[2]
    d_gmlp = gmlp_norm_g.shape[1]
    d_expert = w_down.shape[2]
    n_experts = w_router.shape[2]
    cap = CAPACITY_FACTOR * seq // n_experts
    t = batch * seq
    o_dt = d_ssd + conv_ch
    o_u = o_dt + 2 * n_heads

    xf = x.reshape(t, d_model)
    memf = mem.reshape(batch * mem_len, d_model)
    w_in_t = jnp.swapaxes(w_in, 1, 2).reshape(depth * w_in.shape[2], d_model)
    for l in range(depth):
        h = norm_cast(xf, norm_mix_g[l])
        r0 = l * w_in.shape[2]
        proj = matmul_nt(h, w_in_t, r0, o_dt)
        proj_uv = matmul_nt(h, w_in_t, r0 + o_u, 2 * d_gmlp)
        dt_raw = matmul_nt(h, w_in_t, r0 + o_dt, LANE, out_dtype=F32)
        xbc = conv_silu(proj, conv_w[l], conv_b[l], batch=batch, seq=seq, col0=d_ssd)
        y_f, y_b = ssd_scan(xbc, dt_raw, dt_bias[l], a_log[l], batch=batch, seq=seq, d_ssd=d_ssd)
        y_cat = mixer_out(proj, proj_uv, xbc, y_f, y_b, d_skip[l], ssd_norm_g[l], gmlp_norm_g[l],
                          gmlp_ws[l], gmlp_bs[l], d_ssd=d_ssd, d_gmlp=d_gmlp)
        xf = matmul(y_cat, w_out, l, res=xf)

        q = matmul(norm_cast(xf, norm_xattn_g[l]), w_q, l)
        kv = matmul(norm_cast(memf, norm_mem_g[l]), w_kv, l)
        o = cross_attention(q, kv, batch=batch, seq=seq, mem_len=mem_len)
        xf = matmul(o, w_o, l, res=xf)

        aff_t, aff_n = router_affinity(xf, norm_moe_g[l], w_router, l, batch=batch, seq=seq)
        idx, gate = route(aff_t, aff_n, cap=cap)
        idx_local = jnp.swapaxes(idx, 0, 1)
        idx_flat = (idx_local + (jnp.arange(batch, dtype=jnp.int32) * seq)[None, :, None]).reshape(-1)
        gates = jnp.swapaxes(gate, 0, 1).reshape(n_experts, batch * cap, 1)
        act = expert_up(idx_flat, xf, norm_moe_g[l], w_gate_up, l, rows=batch * cap, d_expert=d_expert)
        y = expert_down(act, w_down, gates, l)
        xf = moe_combine(idx_local.reshape(-1), y, xf, batch=batch, seq=seq, cap=cap)
    out = final_norm(xf, final_norm_g)
    return out.reshape(batch, seq, d_model)
```

```python
import functools

import jax
import jax.numpy as jnp
from jax import lax
from jax.experimental import pallas as pl
from jax.experimental.pallas import tpu as pltpu

F32 = jnp.float32
BF16 = jnp.bfloat16

RMS_EPS = 1e-6
SSD_HEAD_DIM = 64
SSD_GROUPS = 8
D_STATE = 128
D_CONV = 5
CHUNK = 128
GMLP_GROUP_WIDTH = 128
XATTN_HEADS = 4
N_EXPERTS = 16
CAPACITY_FACTOR = 2
LANE = 128
VMEM_LIMIT = 56 * 1024 * 1024


def _cparams(sem, vmem=VMEM_LIMIT):
    return pltpu.CompilerParams(dimension_semantics=sem, vmem_limit_bytes=vmem)


def _silu(x):
    return x * (1.0 / (1.0 + jnp.exp(-x)))


def _rms_rows(x, g):
    ms = jnp.mean(x * x, axis=-1, keepdims=True)
    return x * lax.rsqrt(ms + RMS_EPS) * g


def _norm_cast_kernel(x_ref, g_ref, o_ref):
    o_ref[...] = _rms_rows(x_ref[...], g_ref[...]).astype(o_ref.dtype)


def norm_cast(x, g, *, tm=512):
    m, k = x.shape
    tm = min(tm, m)
    return pl.pallas_call(
        _norm_cast_kernel,
        out_shape=jax.ShapeDtypeStruct((m, k), BF16),
        grid=(m // tm,),
        in_specs=[pl.BlockSpec((tm, k), lambda i: (i, 0)), pl.BlockSpec((1, k), lambda i: (0, 0))],
        out_specs=pl.BlockSpec((tm, k), lambda i: (i, 0)),
        compiler_params=_cparams(("parallel",)),
        name="norm_cast",
    )(x, g.reshape(1, k).astype(F32))


def _mm_kernel(a_ref, w_ref, o_ref):
    o_ref[...] = jnp.dot(a_ref[...], w_ref[...].astype(BF16),
                         preferred_element_type=F32).astype(o_ref.dtype)


def _mm_nt_kernel(a_ref, wt_ref, o_ref):
    o_ref[...] = lax.dot_general(a_ref[...], wt_ref[...].astype(BF16), (((1,), (1,)), ((), ())),
                                 preferred_element_type=F32).astype(o_ref.dtype)


def matmul_nt(a, wt, row0, n, *, out_dtype=BF16, tm=1024, tn=512):
    m, k = a.shape
    tm = min(tm, m)
    tn = min(tn, n)
    assert m % tm == 0 and n % tn == 0 and row0 % 8 == 0 and wt.shape[1] == k
    return pl.pallas_call(
        _mm_nt_kernel,
        out_shape=jax.ShapeDtypeStruct((m, n), out_dtype),
        grid=(m // tm, n // tn),
        in_specs=[pl.BlockSpec((tm, k), lambda i, j: (i, 0)),
                  pl.BlockSpec((pl.Element(tn), pl.Element(k)),
                               lambda i, j: (pl.multiple_of(row0 + j * tn, 8), 0))],
        out_specs=pl.BlockSpec((tm, tn), lambda i, j: (i, j)),
        compiler_params=_cparams(("parallel", "arbitrary")),
        name="matmul_nt",
    )(a, wt)


def _mm_res_kernel(a_ref, w_ref, r_ref, o_ref):
    o_ref[...] = r_ref[...] + jnp.dot(a_ref[...], w_ref[...].astype(BF16), preferred_element_type=F32)


def matmul(a, w, layer, *, n_cols=None, res=None, out_dtype=BF16, tm=1024, tn=512):
    m, k = a.shape
    n = w.shape[2] if n_cols is None else n_cols
    tm = min(tm, m)
    tn = min(tn, n)
    assert m % tm == 0 and n % tn == 0
    a_spec = pl.BlockSpec((tm, k), lambda i, j: (i, 0))
    w_spec = pl.BlockSpec((None, k, tn), lambda i, j: (layer, 0, j))
    o_spec = pl.BlockSpec((tm, tn), lambda i, j: (i, j))
    if res is None:
        return pl.pallas_call(
            _mm_kernel,
            out_shape=jax.ShapeDtypeStruct((m, n), out_dtype),
            grid=(m // tm, n // tn),
            in_specs=[a_spec, w_spec],
            out_specs=o_spec,
            compiler_params=_cparams(("parallel", "arbitrary")),
            name="matmul",
        )(a, w)
    return pl.pallas_call(
        _mm_res_kernel,
        out_shape=jax.ShapeDtypeStruct((m, n), F32),
        grid=(m // tm, n // tn),
        in_specs=[a_spec, w_spec, o_spec],
        out_specs=o_spec,
        compiler_params=_cparams(("parallel", "arbitrary")),
        name="matmul_residual",
    )(a, w, res)


HALO = 16


def _conv_kernel(prev_ref, cur_ref, next_ref, w_ref, b_ref, o_ref, ext_ref, *, n_seq_tiles):
    ts = cur_ref.shape[0]
    s = pl.program_id(1)
    prev = prev_ref[...].astype(F32)
    nxt = next_ref[...].astype(F32)
    ext_ref[0:HALO, :] = jnp.where(s == 0, 0.0, prev)
    ext_ref[HALO:HALO + ts, :] = cur_ref[...].astype(F32)
    ext_ref[HALO + ts:, :] = jnp.where(s == n_seq_tiles - 1, 0.0, nxt)
    acc = jnp.zeros(cur_ref.shape, F32) + b_ref[...]
    for k in range(D_CONV):
        off = HALO - D_CONV // 2 + k
        acc = acc + ext_ref[off:off + ts, :] * w_ref[k:k + 1, :]
    o_ref[...] = _silu(acc).astype(o_ref.dtype)


def conv_silu(proj, conv_w, conv_b, *, batch, seq, col0, ts=512, tc=1024):
    c = conv_w.shape[1]
    t = batch * seq
    nst = seq // ts
    hb = ts // HALO
    cb0 = col0 // tc
    assert col0 % tc == 0 and c % tc == 0 and seq % ts == 0
    last_halo = t // HALO - 1

    def prev_map(b, s, j):
        return (jnp.maximum((b * nst + s) * hb - 1, 0), cb0 + j)

    def next_map(b, s, j):
        return (jnp.minimum((b * nst + s + 1) * hb, last_halo), cb0 + j)

    return pl.pallas_call(
        functools.partial(_conv_kernel, n_seq_tiles=nst),
        out_shape=jax.ShapeDtypeStruct((t, c), BF16),
        grid=(batch, nst, c // tc),
        in_specs=[
            pl.BlockSpec((HALO, tc), prev_map),
            pl.BlockSpec((ts, tc), lambda b, s, j: (b * nst + s, cb0 + j)),
            pl.BlockSpec((HALO, tc), next_map),
            pl.BlockSpec((D_CONV, tc), lambda b, s, j: (0, j)),
            pl.BlockSpec((1, tc), lambda b, s, j: (0, j)),
        ],
        out_specs=pl.BlockSpec((ts, tc), lambda b, s, j: (b * nst + s, j)),
        scratch_shapes=[pltpu.VMEM((ts + 2 * HALO, tc), F32)],
        compiler_params=_cparams(("parallel", "parallel", "parallel")),
        name="conv_silu",
    )(proj, proj, proj, conv_w.astype(F32), conv_b.reshape(1, c).astype(F32))


def _ssd_direction(xs_ref, b_ref, c_ref, dt_ref, bias_ref, alog_ref, expand_ref, st_ref, y_ref,
                   *, reverse, col0, n_heads):
    L = CHUNK
    hp = SSD_HEAD_DIM
    heads_per_group = n_heads // SSD_GROUPS
    gw = heads_per_group * hp
    row = lax.broadcasted_iota(jnp.int32, (L, L), 0)
    col = lax.broadcasted_iota(jnp.int32, (L, L), 1)
    keep = (col >= row) if reverse else (col <= row)
    tri = jnp.where(keep, 1.0, 0.0).astype(F32)

    xraw = dt_ref[...] + bias_ref[...]
    dt = jnp.maximum(xraw, 0.0) + jnp.log1p(jnp.exp(-jnp.abs(xraw)))
    a = -jnp.exp(alog_ref[...])
    dta = dt * a
    acs = jnp.dot(tri, dta, preferred_element_type=F32, precision=lax.Precision.HIGHEST)
    acs_t = acs.T
    total = acs[0:1, :] if reverse else acs[L - 1:L, :]
    dte = jnp.exp(total - acs)
    ea = jnp.exp(acs)
    cdec = jnp.broadcast_to(jnp.exp(total), (8, LANE))

    stack = jnp.concatenate([dt, dte, ea, cdec], axis=0).astype(BF16)
    ex = jnp.dot(stack, expand_ref[...], preferred_element_type=F32)
    dt_x = ex[0:L]
    dte_x = ex[L:2 * L]
    ea_x = ex[2 * L:3 * L]
    cdec_x = ex[3 * L:3 * L + 1]

    lane_head = lax.broadcasted_iota(jnp.int32, (L, gw), 1) // hp
    for g in range(SSD_GROUPS):
        sl = slice(g * gw, (g + 1) * gw)
        xs_g = xs_ref[:, sl].astype(F32)
        xdt = xs_g * dt_x[:, sl]
        xdt_bf = xdt.astype(BF16)
        xdtw_bf = (xdt * dte_x[:, sl]).astype(BF16)
        b_g = b_ref[:, g * D_STATE:(g + 1) * D_STATE]
        c_g = c_ref[:, g * D_STATE:(g + 1) * D_STATE]
        cb = lax.dot_general(c_g, b_g, (((1,), (1,)), ((), ())), preferred_element_type=F32)
        y_g = jnp.zeros((L, gw), F32)
        for r in range(heads_per_group):
            hc = col0 + g * heads_per_group + r
            seg = acs[:, hc:hc + 1] - acs_t[hc:hc + 1, :]
            decay = jnp.exp(jnp.where(keep, seg, -jnp.inf))
            m = (cb * decay).astype(BF16)
            x_r = jnp.where(lane_head == r, xdt_bf, jnp.zeros_like(xdt_bf))
            y_g = y_g + jnp.dot(m, x_r, preferred_element_type=F32)
        st = st_ref[g]
        y_off = jnp.dot(c_g, st.astype(BF16), preferred_element_type=F32) * ea_x[:, sl]
        y_ref[:, sl] = (y_g + y_off).astype(y_ref.dtype)
        b_t = b_g.astype(F32).T.astype(BF16)
        st_ref[g] = st * cdec_x[:, sl] + jnp.dot(b_t, xdtw_bf, preferred_element_type=F32)


def _ssd_kernel(xs_f, b_f, c_f, dt_f, xs_b, b_b, c_b, dt_b, bias_ref, alog_ref, expand_ref,
                yf_ref, yb_ref, stf_ref, stb_ref, *, n_heads):
    @pl.when(pl.program_id(1) == 0)
    def _():
        stf_ref[...] = jnp.zeros_like(stf_ref)
        stb_ref[...] = jnp.zeros_like(stb_ref)

    _ssd_direction(xs_f, b_f, c_f, dt_f, bias_ref, alog_ref, expand_ref.at[0], stf_ref, yf_ref,
                   reverse=False, col0=0, n_heads=n_heads)
    _ssd_direction(xs_b, b_b, c_b, dt_b, bias_ref, alog_ref, expand_ref.at[1], stb_ref, yb_ref,
                   reverse=True, col0=n_heads, n_heads=n_heads)


def ssd_scan(xbc, dt_raw, dt_bias, a_log, *, batch, seq, d_ssd):
    t = batch * seq
    nc = seq // CHUNK
    n_heads = d_ssd // SSD_HEAD_DIM
    gn = SSD_GROUPS * D_STATE
    assert d_ssd % gn == 0 and 2 * n_heads <= LANE
    xb = d_ssd // gn
    bias = jnp.zeros((1, LANE), F32).at[0, :2 * n_heads].set(dt_bias.reshape(-1).astype(F32))
    alog = jnp.zeros((1, LANE), F32).at[0, :2 * n_heads].set(a_log.reshape(-1).astype(F32))
    lane_h = jnp.arange(d_ssd) // SSD_HEAD_DIM
    expand = jnp.stack([
        (jnp.arange(LANE)[:, None] == (d * n_heads + lane_h)[None, :]) for d in range(2)
    ]).astype(BF16)

    def fwd(b, c):
        return b * nc + c

    def bwd(b, c):
        return b * nc + (nc - 1 - c)

    def specs(rmap):
        return [
            pl.BlockSpec((CHUNK, d_ssd), lambda b, c: (rmap(b, c), 0)),
            pl.BlockSpec((CHUNK, gn), lambda b, c: (rmap(b, c), xb)),
            pl.BlockSpec((CHUNK, gn), lambda b, c: (rmap(b, c), xb + 1)),
            pl.BlockSpec((CHUNK, LANE), lambda b, c: (rmap(b, c), 0)),
        ]

    const2 = lambda b, c: (0, 0)
    gw = d_ssd // SSD_GROUPS
    return pl.pallas_call(
        functools.partial(_ssd_kernel, n_heads=n_heads),
        out_shape=(jax.ShapeDtypeStruct((t, d_ssd), BF16), jax.ShapeDtypeStruct((t, d_ssd), BF16)),
        grid=(batch, nc),
        in_specs=specs(fwd) + specs(bwd) + [
            pl.BlockSpec((1, LANE), const2),
            pl.BlockSpec((1, LANE), const2),
            pl.BlockSpec((2, LANE, d_ssd), lambda b, c: (0, 0, 0)),
        ],
        out_specs=(
            pl.BlockSpec((CHUNK, d_ssd), lambda b, c: (fwd(b, c), 0)),
            pl.BlockSpec((CHUNK, d_ssd), lambda b, c: (bwd(b, c), 0)),
        ),
        scratch_shapes=[pltpu.VMEM((SSD_GROUPS, D_STATE, gw), F32),
                        pltpu.VMEM((SSD_GROUPS, D_STATE, gw), F32)],
        compiler_params=_cparams(("parallel", "arbitrary")),
        name="ssd_scan",
    )(xbc, xbc, xbc, dt_raw, xbc, xbc, xbc, dt_raw, bias, alog, expand)


def _gelu(x):
    return 0.5 * x * (1.0 + lax.erf(x * (2.0 ** -0.5)))


def _mixer_out_kernel(z_ref, xs_ref, yf_ref, yb_ref, u_ref, v_ref, dskip_ref, sg_ref, gg_ref,
                      ws_ref, bs_ref, o_ref, *, d_ssd):
    gw = d_ssd // SSD_GROUPS
    y = (yf_ref[...].astype(F32) + yb_ref[...].astype(F32)
         + xs_ref[...].astype(F32) * dskip_ref[...]) * _silu(z_ref[...].astype(F32))
    for g in range(SSD_GROUPS):
        sl = slice(g * gw, (g + 1) * gw)
        o_ref[:, sl] = _rms_rows(y[:, sl], sg_ref[:, sl]).astype(o_ref.dtype)

    uu = _gelu(u_ref[...].astype(F32))
    vn = _rms_rows(_gelu(v_ref[...].astype(F32)), gg_ref[...]).astype(BF16)
    bs = bs_ref[...]
    for g in range(ws_ref.shape[0]):
        sl = slice(g * GMLP_GROUP_WIDTH, (g + 1) * GMLP_GROUP_WIDTH)
        sp = jnp.dot(ws_ref[g], vn[:, sl], preferred_element_type=F32) + bs[:, g:g + 1]
        o_ref[:, d_ssd + g * GMLP_GROUP_WIDTH:d_ssd + (g + 1) * GMLP_GROUP_WIDTH] = (
            uu[:, sl] * sp).astype(o_ref.dtype)


def mixer_out(proj, proj_uv, xbc, y_f, y_b, d_skip, ssd_norm_g, gmlp_norm_g, gmlp_ws, gmlp_bs, *, d_ssd,
              d_gmlp):
    t = proj.shape[0]
    assert d_ssd == d_gmlp
    w = d_ssd
    n_groups = gmlp_ws.shape[0]
    dskip = jnp.repeat(d_skip.astype(F32), SSD_HEAD_DIM).reshape(1, d_ssd)
    row = lambda i: (i, 0)
    const = lambda i: (0, 0)
    return pl.pallas_call(
        functools.partial(_mixer_out_kernel, d_ssd=d_ssd),
        out_shape=jax.ShapeDtypeStruct((t, d_ssd + d_gmlp), BF16),
        grid=(t // CHUNK,),
        in_specs=[
            pl.BlockSpec((CHUNK, w), row),
            pl.BlockSpec((CHUNK, w), row),
            pl.BlockSpec((CHUNK, w), row),
            pl.BlockSpec((CHUNK, w), row),
            pl.BlockSpec((CHUNK, w), row),
            pl.BlockSpec((CHUNK, w), lambda i: (i, 1)),
            pl.BlockSpec((1, w), const),
            pl.BlockSpec((1, w), const),
            pl.BlockSpec((1, w), const),
            pl.BlockSpec((n_groups, CHUNK, CHUNK), lambda i: (0, 0, 0)),
            pl.BlockSpec((CHUNK, n_groups), const),
        ],
        out_specs=pl.BlockSpec((CHUNK, d_ssd + d_gmlp), row),
        compiler_params=_cparams(("parallel",)),
        name="mixer_out",
    )(proj, xbc, y_f, y_b, proj_uv, proj_uv, dskip, ssd_norm_g.reshape(1, -1).astype(F32),
      gmlp_norm_g.reshape(1, -1).astype(F32), gmlp_ws.astype(BF16), gmlp_bs.T.astype(F32))


def _xattn_kernel(q_ref, k_ref, v_ref, o_ref, *, scale):
    s = lax.dot_general(q_ref[...], k_ref[...], (((1,), (1,)), ((), ())),
                        preferred_element_type=F32) * scale
    p = jnp.exp(s - jnp.max(s, axis=-1, keepdims=True))
    denom = jnp.sum(p, axis=-1, keepdims=True)
    o = jnp.dot(p.astype(BF16), v_ref[...], preferred_element_type=F32)
    o_ref[...] = (o / denom).astype(o_ref.dtype)


def cross_attention(q, kv, *, batch, seq, mem_len, tq=1024):
    t, d = q.shape
    hd = d // XATTN_HEADS
    nq = seq // tq
    return pl.pallas_call(
        functools.partial(_xattn_kernel, scale=hd ** -0.5),
        out_shape=jax.ShapeDtypeStruct((t, d), BF16),
        grid=(batch, XATTN_HEADS, nq),
        in_specs=[
            pl.BlockSpec((tq, hd), lambda b, h, i: (b * nq + i, h)),
            pl.BlockSpec((mem_len, hd), lambda b, h, i: (b, h)),
            pl.BlockSpec((mem_len, hd), lambda b, h, i: (b, XATTN_HEADS + h)),
        ],
        out_specs=pl.BlockSpec((tq, hd), lambda b, h, i: (b * nq + i, h)),
        compiler_params=_cparams(("parallel", "parallel", "parallel")),
        name="cross_attention",
    )(q, kv, kv)


def _router_kernel(x_ref, g_ref, whi_ref, wlo_ref, afft_ref, affn_ref, *, n_experts):
    h = _rms_rows(x_ref[...], g_ref[...])
    h_hi = h.astype(BF16)
    h_lo = (h - h_hi.astype(F32)).astype(BF16)
    logits = (jnp.dot(h_hi, whi_ref[...], preferred_element_type=F32)
              + jnp.dot(h_lo, whi_ref[...], preferred_element_type=F32)
              + jnp.dot(h_hi, wlo_ref[...], preferred_element_type=F32))
    lane = lax.broadcasted_iota(jnp.int32, logits.shape, 1)
    valid = lane < n_experts
    lm = jnp.where(valid, logits, -jnp.inf)
    q = jnp.where(valid, jnp.exp(lm - jnp.max(lm, axis=1, keepdims=True)), 0.0)
    aff = q / jnp.sum(q, axis=1, keepdims=True)
    affn_ref[...] = aff
    afft_ref[...] = aff.T[0:n_experts, :]


def router_affinity(x, g, w_router, layer, *, batch, seq, tm=512):
    t, d = x.shape
    e = w_router.shape[2]
    w_pad = jnp.zeros((d, LANE), F32).at[:, :e].set(w_router[layer].astype(F32))
    w_hi = w_pad.astype(BF16)
    w_lo = (w_pad - w_hi.astype(F32)).astype(BF16)
    ns = seq // tm
    return pl.pallas_call(
        functools.partial(_router_kernel, n_experts=e),
        out_shape=(jax.ShapeDtypeStruct((batch, e, seq), F32), jax.ShapeDtypeStruct((t, LANE), F32)),
        grid=(batch, ns),
        in_specs=[
            pl.BlockSpec((tm, d), lambda b, i: (b * ns + i, 0)),
            pl.BlockSpec((1, d), lambda b, i: (0, 0)),
            pl.BlockSpec((d, LANE), lambda b, i: (0, 0)),
            pl.BlockSpec((d, LANE), lambda b, i: (0, 0)),
        ],
        out_specs=(pl.BlockSpec((None, e, tm), lambda b, i: (b, 0, i)),
                   pl.BlockSpec((tm, LANE), lambda b, i: (b * ns + i, 0))),
        compiler_params=_cparams(("parallel", "parallel")),
        name="router",
    )(x, g.reshape(1, d).astype(F32), w_hi, w_lo)


BISECT_ITERS = 152


def _route_kernel(afft_ref, affn_ref, idx_ref, gate_ref, slot_ref, *, cap):
    n_exp, seq = afft_ref.shape
    blk = LANE
    nblk = seq // blk
    capf = float(cap)

    xt = afft_ref[...]

    def bisect(_, carry):
        lo, hi = carry
        mid = 0.5 * (lo + hi)
        cnt = jnp.sum(jnp.where(xt >= mid, 1.0, 0.0), axis=1, keepdims=True)
        ge = cnt >= capf
        return jnp.where(ge, mid, lo), jnp.where(ge, hi, mid)

    lo, hi = lax.fori_loop(0, BISECT_ITERS, bisect,
                           (jnp.zeros((n_exp, 1), F32), jnp.full((n_exp, 1), 2.0, F32)))
    r_i = lax.broadcasted_iota(jnp.int32, (n_exp, LANE), 0)
    c_i = lax.broadcasted_iota(jnp.int32, (n_exp, LANE), 1)
    diag = r_i == c_i
    lane_ok = lax.broadcasted_iota(jnp.int32, (1, LANE), 1) < n_exp
    lo_r = jnp.where(lane_ok, jnp.sum(jnp.where(diag, lo, 0.0), axis=0, keepdims=True), 4.0)
    hi_r = jnp.where(lane_ok, jnp.sum(jnp.where(diag, hi, 0.0), axis=0, keepdims=True), 4.0)

    n_gt = jnp.sum(jnp.where(affn_ref[...] >= hi_r, 1.0, 0.0), axis=0, keepdims=True)
    need = capf - n_gt

    tr = lax.broadcasted_iota(jnp.int32, (blk, blk), 0)
    tc = lax.broadcasted_iota(jnp.int32, (blk, blk), 1)
    tril = jnp.where(tr >= tc, 1.0, 0.0).astype(BF16)
    carry_eq = jnp.zeros((1, LANE), F32)
    carry_sel = jnp.zeros((1, LANE), F32)
    for k in range(nblk):
        x = affn_ref[k * blk:(k + 1) * blk, :]
        gt = x >= hi_r
        eq = jnp.logical_and(x >= lo_r, jnp.logical_not(gt))
        pos_eq = jnp.dot(tril, jnp.where(eq, 1.0, 0.0).astype(BF16), preferred_element_type=F32) + carry_eq
        carry_eq = pos_eq[blk - 1:blk, :]
        sel = jnp.logical_or(gt, jnp.logical_and(eq, pos_eq <= need))
        cs = jnp.dot(tril, jnp.where(sel, 1.0, 0.0).astype(BF16), preferred_element_type=F32) + carry_sel
        carry_sel = cs[blk - 1:blk, :]
        slot_ref[k * blk:(k + 1) * blk, :] = jnp.where(sel, cs - 1.0, -1.0)

    s_iota = lax.broadcasted_iota(jnp.int32, (blk, cap), 1).astype(F32)
    t_iota = lax.broadcasted_iota(jnp.int32, (blk, 1), 0).astype(F32)
    for e in range(n_exp):
        def body(k, acc):
            acc_i, acc_g = acc
            r0 = pl.multiple_of(k * blk, blk)
            slot = slot_ref[pl.ds(r0, blk), :][:, e:e + 1]
            aff = affn_ref[pl.ds(r0, blk), :][:, e:e + 1]
            hit = jnp.broadcast_to(slot, (blk, cap)) == s_iota
            tok = t_iota + lax.convert_element_type(k * blk, F32)
            acc_i = acc_i + jnp.sum(jnp.where(hit, tok, 0.0), axis=0, keepdims=True)
            acc_g = acc_g + jnp.sum(jnp.where(hit, aff, 0.0), axis=0, keepdims=True)
            return acc_i, acc_g

        acc_i, acc_g = lax.fori_loop(0, nblk, body, (jnp.zeros((1, cap), F32), jnp.zeros((1, cap), F32)))
        idx_ref[e:e + 1, :] = acc_i.astype(jnp.int32)
        gate_ref[e:e + 1, :] = acc_g


def route(aff_t, aff_n, *, cap):
    batch, e, seq = aff_t.shape
    return pl.pallas_call(
        functools.partial(_route_kernel, cap=cap),
        out_shape=(jax.ShapeDtypeStruct((batch, e, cap), jnp.int32),
                   jax.ShapeDtypeStruct((batch, e, cap), F32)),
        grid=(batch,),
        in_specs=[pl.BlockSpec((None, e, seq), lambda b: (b, 0, 0)),
                  pl.BlockSpec((seq, LANE), lambda b: (b, 0))],
        out_specs=(pl.BlockSpec((None, e, cap), lambda b: (b, 0, 0)),
                   pl.BlockSpec((None, e, cap), lambda b: (b, 0, 0))),
        scratch_shapes=[pltpu.VMEM((seq, LANE), F32)],
        compiler_params=_cparams(("parallel",)),
        name="route",
    )(aff_t, aff_n)


def _row_copy(src_ref, src_row, dst_ref, dst_row, sem):
    return pltpu.make_async_copy(src_ref.at[pl.ds(src_row, 1)], dst_ref.at[pl.ds(dst_row, 1)], sem)


_GATHER_SLICE = 176


def _expert_up_kernel(idx_ref, x_hbm, g_ref, wg_ref, wu_ref, o_ref, stage_ref, xn_ref, sem):
    e = pl.program_id(0)
    f = pl.program_id(1)
    n_exp = pl.num_programs(0)
    nf = pl.num_programs(1)
    rows = xn_ref.shape[0]

    def issue_slice(expert, step):
        for u in range(_GATHER_SLICE):
            r = step * _GATHER_SLICE + u
            src = idx_ref[expert * rows + jnp.minimum(r, rows - 1)]
            _row_copy(x_hbm, src, stage_ref, r, sem).start(priority=u % 2)

    def wait_gather():
        pltpu.make_async_copy(x_hbm.at[pl.ds(0, stage_ref.shape[0])], stage_ref, sem).wait()

    @pl.when(f == 0)
    def _():
        @pl.when(e == 0)
        def _():
            def first(step, c):
                issue_slice(0, step)
                return c

            lax.fori_loop(0, nf, first, 0)

        wait_gather()

        def norm(i, c):
            r = pl.multiple_of(i * 32, 32)
            xn_ref[pl.ds(r, 32), :] = _rms_rows(stage_ref[pl.ds(r, 32), :], g_ref[...]).astype(xn_ref.dtype)
            return c

        lax.fori_loop(0, rows // 32, norm, 0)

    issue_slice(lax.rem(e + 1, n_exp), f)

    xn = xn_ref[...]
    gate = jnp.dot(xn, wg_ref[...].astype(BF16), preferred_element_type=F32)
    up = jnp.dot(xn, wu_ref[...].astype(BF16), preferred_element_type=F32)
    o_ref[...] = (_silu(gate) * up).astype(o_ref.dtype)

    @pl.when(jnp.logical_and(e == n_exp - 1, f == nf - 1))
    def _():
        wait_gather()


def expert_up(idx_flat, x, g, w_gate_up, layer, *, rows, d_expert, tf=256):
    t, d = x.shape
    e = w_gate_up.shape[1]
    nf = d_expert // tf
    assert _GATHER_SLICE * nf >= rows and _GATHER_SLICE % 8 == 0
    grid_spec = pltpu.PrefetchScalarGridSpec(
        num_scalar_prefetch=1,
        grid=(e, nf),
        in_specs=[
            pl.BlockSpec(memory_space=pl.ANY),
            pl.BlockSpec((1, d), lambda i, f, idx: (0, 0)),
            pl.BlockSpec((None, None, d, tf), lambda i, f, idx: (layer, i, 0, f)),
            pl.BlockSpec((None, None, d, tf), lambda i, f, idx: (layer, i, 0, nf + f)),
        ],
        out_specs=pl.BlockSpec((None, rows, tf), lambda i, f, idx: (i, 0, f)),
        scratch_shapes=[
            pltpu.VMEM((_GATHER_SLICE * nf, d), F32),
            pltpu.VMEM((rows, d), BF16),
            pltpu.SemaphoreType.DMA(()),
        ],
    )
    return pl.pallas_call(
        _expert_up_kernel,
        out_shape=jax.ShapeDtypeStruct((e, rows, d_expert), BF16),
        grid_spec=grid_spec,
        compiler_params=_cparams(("arbitrary", "arbitrary")),
        name="expert_up",
    )(idx_flat, x, g.reshape(1, d).astype(F32), w_gate_up, w_gate_up)


def _expert_down_kernel(act_ref, w_ref, gate_ref, o_ref):
    o_ref[...] = (jnp.dot(act_ref[...], w_ref[...].astype(BF16), preferred_element_type=F32)
                  * gate_ref[...]).astype(o_ref.dtype)


def expert_down(act, w_down, gates, layer, *, tn=512):
    e, rows, f = act.shape
    d = w_down.shape[3]
    return pl.pallas_call(
        _expert_down_kernel,
        out_shape=jax.ShapeDtypeStruct((e, rows, d), BF16),
        grid=(e, d // tn),
        in_specs=[
            pl.BlockSpec((None, rows, f), lambda i, j: (i, 0, 0)),
            pl.BlockSpec((None, None, f, tn), lambda i, j: (layer, i, 0, j)),
            pl.BlockSpec((None, rows, 1), lambda i, j: (i, 0, 0)),
        ],
        out_specs=pl.BlockSpec((None, rows, tn), lambda i, j: (i, 0, j)),
        compiler_params=_cparams(("parallel", "arbitrary")),
        name="expert_down",
    )(act, w_down, gates)


SUBLANE = 8
COMBINE_CHUNK = 512
COMBINE_UNROLL = 16


def _combine_kernel(idx_ref, y_ref, x_hbm, o_hbm, acc_ref, y3_ref, stage_ref, sem_in, sem_out,
                    *, seq, cap, n_batch):
    n = pl.program_id(0)
    b = pl.program_id(1)
    e = pl.program_id(2)
    n_exp = pl.num_programs(2)
    cn = y_ref.shape[1]
    ch = stage_ref.shape[1]
    n_chunks = seq // ch
    col = pl.multiple_of(n * cn, cn)

    def x_copy(c):
        row = pl.multiple_of(b * seq + c * ch, ch)
        return pltpu.make_async_copy(x_hbm.at[pl.ds(row, ch), pl.ds(col, cn)], stage_ref.at[c % 2],
                                     sem_in.at[c % 2])

    def o_copy(c):
        row = pl.multiple_of(b * seq + c * ch, ch)
        return pltpu.make_async_copy(stage_ref.at[c % 2], o_hbm.at[pl.ds(row, ch), pl.ds(col, cn)],
                                     sem_out.at[c % 2])

    @pl.when(e == 0)
    def _():
        x_copy(0).start()
        for c in range(n_chunks):
            x_copy(c).wait()
            if c + 1 < n_chunks:
                x_copy(c + 1).start()
            acc_ref[c * ch:(c + 1) * ch] = stage_ref[c % 2].reshape(ch, SUBLANE, cn // SUBLANE)

    y3_ref[...] = y_ref[...].astype(F32).reshape(cap, SUBLANE, cn // SUBLANE)
    base = (e * n_batch + b) * cap

    def body(i, c):
        toks = [idx_ref[base + COMBINE_UNROLL * i + u] for u in range(COMBINE_UNROLL)]
        vals = [acc_ref[toks[u]] + y3_ref[COMBINE_UNROLL * i + u] for u in range(COMBINE_UNROLL)]
        for u in range(COMBINE_UNROLL):
            acc_ref[toks[u]] = vals[u]
        return c

    lax.fori_loop(0, cap // COMBINE_UNROLL, body, 0)

    @pl.when(e == n_exp - 1)
    def _():
        for c in range(n_chunks):
            if c >= 2:
                o_copy(c - 2).wait()
            stage_ref[c % 2] = acc_ref[c * ch:(c + 1) * ch].reshape(ch, cn)
            o_copy(c).start()
        for c in range(max(n_chunks - 2, 0), n_chunks):
            o_copy(c).wait()


def moe_combine(idx_local, y, x, *, batch, seq, cap, cn=1024):
    e, rows, d = y.shape
    t = x.shape[0]
    grid_spec = pltpu.PrefetchScalarGridSpec(
        num_scalar_prefetch=1,
        grid=(d // cn, batch, e),
        in_specs=[
            pl.BlockSpec((None, cap, cn), lambda n, b, i, idx: (i, b, n)),
            pl.BlockSpec(memory_space=pl.ANY),
        ],
        out_specs=pl.BlockSpec(memory_space=pl.ANY),
        scratch_shapes=[
            pltpu.VMEM((seq, SUBLANE, cn // SUBLANE), F32),
            pltpu.VMEM((cap, SUBLANE, cn // SUBLANE), F32),
            pltpu.VMEM((2, min(COMBINE_CHUNK, seq), cn), F32),
            pltpu.SemaphoreType.DMA((2,)),
            pltpu.SemaphoreType.DMA((2,)),
        ],
    )
    return pl.pallas_call(
        functools.partial(_combine_kernel, seq=seq, cap=cap, n_batch=batch),
        out_shape=jax.ShapeDtypeStruct((t, d), F32),
        grid_spec=grid_spec,
        compiler_params=_cparams(("arbitrary", "arbitrary", "arbitrary")),
        name="moe_combine",
    )(idx_local, y, x)


def _final_norm_kernel(x_ref, g_ref, o_ref):
    o_ref[...] = _rms_rows(x_ref[...], g_ref[...])


def final_norm(x, g, *, tm=512):
    t, d = x.shape
    return pl.pallas_call(
        _final_norm_kernel,
        out_shape=jax.ShapeDtypeStruct((t, d), F32),
        grid=(t // tm,),
        in_specs=[pl.BlockSpec((tm, d), lambda i: (i, 0)), pl.BlockSpec((1, d), lambda i: (0, 0))],
        out_specs=pl.BlockSpec((tm, d), lambda i: (i, 0)),
        compiler_params=_cparams(("parallel",)),
        name="final_norm",
    )(x, g.reshape(1, d).astype(F32))


def kernel(x, mem, norm_mix_g, w_in, conv_w, conv_b, dt_bias, a_log, d_skip, ssd_norm_g, gmlp_norm_g,
           gmlp_ws, gmlp_bs, w_out, norm_xattn_g, norm_mem_g, w_q, w_kv, w_o, norm_moe_g, w_router,
           w_gate_up, w_down, final_norm_g):
    batch, seq, d_model = x.shape
    mem_len = mem.shape[1]
    depth = w_in.shape[0]
    n_heads = dt_bias.shape[2]
    d_ssd = n_heads * SSD_HEAD_DIM
    conv_ch = conv_w.shape[2]
    d_gmlp = gmlp_norm_g.shape[1]
    d_expert = w_down.shape[2]
    n_experts = w_router.shape[2]
    cap = CAPACITY_FACTOR * seq // n_experts
    t = batch * seq
    o_dt = d_ssd + conv_ch
    o_u = o_dt + 2 * n_heads

    xf = x.reshape(t, d_model)
    memf = mem.reshape(batch * mem_len, d_model)
    w_in_t = jnp.swapaxes(w_in, 1, 2).reshape(depth * w_in.shape[2], d_model)
    for l in range(depth):
        h = norm_cast(xf, norm_mix_g[l])
        r0 = l * w_in.shape[2]
        proj = matmul_nt(h, w_in_t, r0, o_dt)
        proj_uv = matmul_nt(h, w_in_t, r0 + o_u, 2 * d_gmlp)
        dt_raw = matmul_nt(h, w_in_t, r0 + o_dt, LANE, out_dtype=F32)
        xbc = conv_silu(proj, conv_w[l], conv_b[l], batch=batch, seq=seq, col0=d_ssd)
        y_f, y_b = ssd_scan(xbc, dt_raw, dt_bias[l], a_log[l], batch=batch, seq=seq, d_ssd=d_ssd)
        y_cat = mixer_out(proj, proj_uv, xbc, y_f, y_b, d_skip[l], ssd_norm_g[l], gmlp_norm_g[l],
                          gmlp_ws[l], gmlp_bs[l], d_ssd=d_ssd, d_gmlp=d_gmlp)
        xf = matmul(y_cat, w_out, l, res=xf)

        q = matmul(norm_cast(xf, norm_xattn_g[l]), w_q, l)
        kv = matmul(norm_cast(memf, norm_mem_g[l]), w_kv, l)
        o = cross_attention(q, kv, batch=batch, seq=seq, mem_len=mem_len)
        xf = matmul(o, w_o, l, res=xf)

        aff_t, aff_n = router_affinity(xf, norm_moe_g[l], w_router, l, batch=batch, seq=seq)
        idx, gate = route(aff_t, aff_n, cap=cap)
        idx_local = jnp.swapaxes(idx, 0, 1)
        idx_flat = (idx_local + (jnp.arange(batch, dtype=jnp.int32) * seq)[None, :, None]).reshape(-1)
        gates = jnp.swapaxes(gate, 0, 1).reshape(n_experts, batch * cap, 1)
        act = expert_up(idx_flat, xf, norm_moe_g[l], w_gate_up, l, rows=batch * cap, d_expert=d_expert)
        y = expert_down(act, w_down, gates, l)
        xf = moe_combine(idx_local.reshape(-1), y, xf, batch=batch, seq=seq, cap=cap)
    out = final_norm(xf, final_norm_g)
    return out.reshape(batch, seq, d_model)
```

```python
import functools

import jax
import jax.numpy as jnp
from jax import lax
from jax.experimental import pallas as pl
from jax.experimental.pallas import tpu as pltpu

F32 = jnp.float32
BF16 = jnp.bfloat16

RMS_EPS = 1e-6
SSD_HEAD_DIM = 64
SSD_GROUPS = 8
D_STATE = 128
D_CONV = 5
CHUNK = 128
GMLP_GROUP_WIDTH = 128
XATTN_HEADS = 4
N_EXPERTS = 16
CAPACITY_FACTOR = 2
LANE = 128
SUBLANE = 8
VMEM_LIMIT = 56 * 1024 * 1024


def _cparams(sem, vmem=VMEM_LIMIT):
    return pltpu.CompilerParams(dimension_semantics=sem, vmem_limit_bytes=vmem)


def _silu(x):
    return x * (1.0 / (1.0 + jnp.exp(-x)))


def _rms_rows(x, g):
    ms = jnp.mean(x * x, axis=-1, keepdims=True)
    return x * lax.rsqrt(ms + RMS_EPS) * g


def _norm_cast_kernel(x_ref, g_ref, o_ref):
    o_ref[...] = _rms_rows(x_ref[...], g_ref[...]).astype(o_ref.dtype)


def norm_cast(x, g, *, tm=512):
    m, k = x.shape
    tm = min(tm, m)
    return pl.pallas_call(
        _norm_cast_kernel,
        out_shape=jax.ShapeDtypeStruct((m, k), BF16),
        grid=(m // tm,),
        in_specs=[pl.BlockSpec((tm, k), lambda i: (i, 0)), pl.BlockSpec((1, k), lambda i: (0, 0))],
        out_specs=pl.BlockSpec((tm, k), lambda i: (i, 0)),
        compiler_params=_cparams(("parallel",)),
        name="norm_cast",
    )(x, g.reshape(1, k).astype(F32))


def _mm_kernel(a_ref, w_ref, o_ref):
    o_ref[...] = jnp.dot(a_ref[...], w_ref[...].astype(BF16),
                         preferred_element_type=F32).astype(o_ref.dtype)


def _mm_nt_kernel(a_ref, wt_ref, o_ref):
    o_ref[...] = lax.dot_general(a_ref[...], wt_ref[...].astype(BF16), (((1,), (1,)), ((), ())),
                                 preferred_element_type=F32).astype(o_ref.dtype)


def matmul_nt(a, wt, row0, n, *, out_dtype=BF16, tm=1024, tn=512):
    m, k = a.shape
    tm = min(tm, m)
    tn = min(tn, n)
    assert m % tm == 0 and n % tn == 0 and row0 % 8 == 0 and wt.shape[1] == k
    return pl.pallas_call(
        _mm_nt_kernel,
        out_shape=jax.ShapeDtypeStruct((m, n), out_dtype),
        grid=(m // tm, n // tn),
        in_specs=[pl.BlockSpec((tm, k), lambda i, j: (i, 0)),
                  pl.BlockSpec((pl.Element(tn), pl.Element(k)),
                               lambda i, j: (pl.multiple_of(row0 + j * tn, 8), 0))],
        out_specs=pl.BlockSpec((tm, tn), lambda i, j: (i, j)),
        compiler_params=_cparams(("parallel", "arbitrary")),
        name="matmul_nt",
    )(a, wt)


def _mm_res_kernel(a_ref, w_ref, r_ref, o_ref):
    o_ref[...] = r_ref[...] + jnp.dot(a_ref[...], w_ref[...].astype(BF16), preferred_element_type=F32)


def _mm_res_norm_kernel(a_ref, w_ref, r_ref, g_ref, o_ref, xg_ref, ss_ref):
    x = r_ref[...] + jnp.dot(a_ref[...], w_ref[...].astype(BF16), preferred_element_type=F32)
    o_ref[...] = x
    xg_ref[...] = (x * g_ref[...]).astype(xg_ref.dtype)

    @pl.when(pl.program_id(1) == 0)
    def _():
        ss_ref[...] = jnp.zeros_like(ss_ref)

    ss_ref[...] += jnp.sum(x * x, axis=1, keepdims=True)


def _mm_rowscale_kernel(a_ref, w_ref, ss_ref, o_ref):
    inv = lax.rsqrt(ss_ref[...] * (1.0 / a_ref.shape[1]) + RMS_EPS)
    o_ref[...] = (jnp.dot(a_ref[...], w_ref[...].astype(BF16), preferred_element_type=F32)
                  * inv).astype(o_ref.dtype)


def matmul(a, w, layer, *, n_cols=None, res=None, out_dtype=BF16, tm=1024, tn=512, next_gain=None,
           row_ss=None):
    m, k = a.shape
    n = w.shape[2] if n_cols is None else n_cols
    tm = min(tm, m)
    tn = min(tn, n)
    assert m % tm == 0 and n % tn == 0
    a_spec = pl.BlockSpec((tm, k), lambda i, j: (i, 0))
    w_spec = pl.BlockSpec((None, k, tn), lambda i, j: (layer, 0, j))
    o_spec = pl.BlockSpec((tm, tn), lambda i, j: (i, j))
    if next_gain is not None:
        return pl.pallas_call(
            _mm_res_norm_kernel,
            out_shape=(jax.ShapeDtypeStruct((m, n), F32), jax.ShapeDtypeStruct((m, n), BF16),
                       jax.ShapeDtypeStruct((m, 1), F32)),
            grid=(m // tm, n // tn),
            in_specs=[a_spec, w_spec, o_spec, pl.BlockSpec((1, tn), lambda i, j: (0, j))],
            out_specs=(o_spec, o_spec, pl.BlockSpec((tm, 1), lambda i, j: (i, 0))),
            compiler_params=_cparams(("parallel", "arbitrary")),
            name="matmul_residual_norm",
        )(a, w, res, next_gain.reshape(1, n).astype(F32))
    if row_ss is not None:
        return pl.pallas_call(
            _mm_rowscale_kernel,
            out_shape=jax.ShapeDtypeStruct((m, n), out_dtype),
            grid=(m // tm, n // tn),
            in_specs=[a_spec, w_spec, pl.BlockSpec((tm, 1), lambda i, j: (i, 0))],
            out_specs=o_spec,
            compiler_params=_cparams(("parallel", "arbitrary")),
            name="matmul_rowscale",
        )(a, w, row_ss)
    if res is None:
        return pl.pallas_call(
            _mm_kernel,
            out_shape=jax.ShapeDtypeStruct((m, n), out_dtype),
            grid=(m // tm, n // tn),
            in_specs=[a_spec, w_spec],
            out_specs=o_spec,
            compiler_params=_cparams(("parallel", "arbitrary")),
            name="matmul",
        )(a, w)
    return pl.pallas_call(
        _mm_res_kernel,
        out_shape=jax.ShapeDtypeStruct((m, n), F32),
        grid=(m // tm, n // tn),
        in_specs=[a_spec, w_spec, o_spec],
        out_specs=o_spec,
        compiler_params=_cparams(("parallel", "arbitrary")),
        name="matmul_residual",
    )(a, w, res)


HALO = 16


def _conv_kernel(prev_ref, cur_ref, next_ref, w_ref, b_ref, o_ref, ext_ref, *, n_seq_tiles):
    ts = cur_ref.shape[0]
    s = pl.program_id(1)
    prev = prev_ref[...].astype(F32)
    nxt = next_ref[...].astype(F32)
    ext_ref[0:HALO, :] = jnp.where(s == 0, 0.0, prev)
    ext_ref[HALO:HALO + ts, :] = cur_ref[...].astype(F32)
    ext_ref[HALO + ts:, :] = jnp.where(s == n_seq_tiles - 1, 0.0, nxt)
    acc = jnp.zeros(cur_ref.shape, F32) + b_ref[...]
    for k in range(D_CONV):
        off = HALO - D_CONV // 2 + k
        acc = acc + ext_ref[off:off + ts, :] * w_ref[k:k + 1, :]
    o_ref[...] = _silu(acc).astype(o_ref.dtype)


def conv_silu(proj, conv_w, conv_b, *, batch, seq, col0, ts=512, tc=1024):
    c = conv_w.shape[1]
    t = batch * seq
    nst = seq // ts
    hb = ts // HALO
    cb0 = col0 // tc
    assert col0 % tc == 0 and c % tc == 0 and seq % ts == 0
    last_halo = t // HALO - 1

    def prev_map(b, s, j):
        return (jnp.maximum((b * nst + s) * hb - 1, 0), cb0 + j)

    def next_map(b, s, j):
        return (jnp.minimum((b * nst + s + 1) * hb, last_halo), cb0 + j)

    return pl.pallas_call(
        functools.partial(_conv_kernel, n_seq_tiles=nst),
        out_shape=jax.ShapeDtypeStruct((t, c), BF16),
        grid=(batch, nst, c // tc),
        in_specs=[
            pl.BlockSpec((HALO, tc), prev_map),
            pl.BlockSpec((ts, tc), lambda b, s, j: (b * nst + s, cb0 + j)),
            pl.BlockSpec((HALO, tc), next_map),
            pl.BlockSpec((D_CONV, tc), lambda b, s, j: (0, j)),
            pl.BlockSpec((1, tc), lambda b, s, j: (0, j)),
        ],
        out_specs=pl.BlockSpec((ts, tc), lambda b, s, j: (b * nst + s, j)),
        scratch_shapes=[pltpu.VMEM((ts + 2 * HALO, tc), F32)],
        compiler_params=_cparams(("parallel", "parallel", "parallel")),
        name="conv_silu",
    )(proj, proj, proj, conv_w.astype(F32), conv_b.reshape(1, c).astype(F32))


def _ssd_direction(xs_ref, b_ref, c_ref, dt_ref, bias_ref, alog_ref, expand_ref, st_ref, y_ref,
                   *, reverse, col0, n_heads):
    L = CHUNK
    hp = SSD_HEAD_DIM
    heads_per_group = n_heads // SSD_GROUPS
    gw = heads_per_group * hp
    row = lax.broadcasted_iota(jnp.int32, (L, L), 0)
    col = lax.broadcasted_iota(jnp.int32, (L, L), 1)
    keep = (col >= row) if reverse else (col <= row)
    tri = jnp.where(keep, 1.0, 0.0).astype(F32)

    xraw = dt_ref[...] + bias_ref[...]
    dt = jnp.maximum(xraw, 0.0) + jnp.log1p(jnp.exp(-jnp.abs(xraw)))
    a = -jnp.exp(alog_ref[...])
    dta = dt * a
    acs = jnp.dot(tri, dta, preferred_element_type=F32, precision=lax.Precision.HIGHEST)
    acs_t = acs.T
    total = acs[0:1, :] if reverse else acs[L - 1:L, :]
    dte = jnp.exp(total - acs)
    ea = jnp.exp(acs)
    cdec = jnp.broadcast_to(jnp.exp(total), (8, LANE))

    stack = jnp.concatenate([dt, dte, ea, cdec], axis=0).astype(BF16)
    ex = jnp.dot(stack, expand_ref[...], preferred_element_type=F32)
    dt_x = ex[0:L]
    dte_x = ex[L:2 * L]
    ea_x = ex[2 * L:3 * L]
    cdec_x = ex[3 * L:3 * L + 1]

    lane_head = lax.broadcasted_iota(jnp.int32, (L, gw), 1) // hp
    for g in range(SSD_GROUPS):
        sl = slice(g * gw, (g + 1) * gw)
        xs_g = xs_ref[:, sl].astype(F32)
        xdt = xs_g * dt_x[:, sl]
        xdt_bf = xdt.astype(BF16)
        xdtw_bf = (xdt * dte_x[:, sl]).astype(BF16)
        b_g = b_ref[:, g * D_STATE:(g + 1) * D_STATE]
        c_g = c_ref[:, g * D_STATE:(g + 1) * D_STATE]
        cb = lax.dot_general(c_g, b_g, (((1,), (1,)), ((), ())), preferred_element_type=F32)
        y_g = jnp.zeros((L, gw), F32)
        for r in range(heads_per_group):
            hc = col0 + g * heads_per_group + r
            seg = acs[:, hc:hc + 1] - acs_t[hc:hc + 1, :]
            decay = jnp.exp(jnp.where(keep, seg, -jnp.inf))
            m = (cb * decay).astype(BF16)
            x_r = jnp.where(lane_head == r, xdt_bf, jnp.zeros_like(xdt_bf))
            y_g = y_g + jnp.dot(m, x_r, preferred_element_type=F32)
        st = st_ref[g]
        y_off = jnp.dot(c_g, st.astype(BF16), preferred_element_type=F32) * ea_x[:, sl]
        y_ref[:, sl] = (y_g + y_off).astype(y_ref.dtype)
        b_t = b_g.astype(F32).T.astype(BF16)
        st_ref[g] = st * cdec_x[:, sl] + jnp.dot(b_t, xdtw_bf, preferred_element_type=F32)


def _ssd_kernel(xs_f, b_f, c_f, dt_f, xs_b, b_b, c_b, dt_b, bias_ref, alog_ref, expand_ref,
                yf_ref, yb_ref, stf_ref, stb_ref, *, n_heads):
    @pl.when(pl.program_id(1) == 0)
    def _():
        stf_ref[...] = jnp.zeros_like(stf_ref)
        stb_ref[...] = jnp.zeros_like(stb_ref)

    _ssd_direction(xs_f, b_f, c_f, dt_f, bias_ref, alog_ref, expand_ref.at[0], stf_ref, yf_ref,
                   reverse=False, col0=0, n_heads=n_heads)
    _ssd_direction(xs_b, b_b, c_b, dt_b, bias_ref, alog_ref, expand_ref.at[1], stb_ref, yb_ref,
                   reverse=True, col0=n_heads, n_heads=n_heads)


def ssd_scan(xbc, dt_raw, dt_bias, a_log, *, batch, seq, d_ssd):
    t = batch * seq
    nc = seq // CHUNK
    n_heads = d_ssd // SSD_HEAD_DIM
    gn = SSD_GROUPS * D_STATE
    assert d_ssd % gn == 0 and 2 * n_heads <= LANE
    xb = d_ssd // gn
    bias = jnp.zeros((1, LANE), F32).at[0, :2 * n_heads].set(dt_bias.reshape(-1).astype(F32))
    alog = jnp.zeros((1, LANE), F32).at[0, :2 * n_heads].set(a_log.reshape(-1).astype(F32))
    lane_h = jnp.arange(d_ssd) // SSD_HEAD_DIM
    expand = jnp.stack([
        (jnp.arange(LANE)[:, None] == (d * n_heads + lane_h)[None, :]) for d in range(2)
    ]).astype(BF16)

    def fwd(b, c):
        return b * nc + c

    def bwd(b, c):
        return b * nc + (nc - 1 - c)

    def specs(rmap):
        return [
            pl.BlockSpec((CHUNK, d_ssd), lambda b, c: (rmap(b, c), 0)),
            pl.BlockSpec((CHUNK, gn), lambda b, c: (rmap(b, c), xb)),
            pl.BlockSpec((CHUNK, gn), lambda b, c: (rmap(b, c), xb + 1)),
            pl.BlockSpec((CHUNK, LANE), lambda b, c: (rmap(b, c), 0)),
        ]

    const2 = lambda b, c: (0, 0)
    gw = d_ssd // SSD_GROUPS
    return pl.pallas_call(
        functools.partial(_ssd_kernel, n_heads=n_heads),
        out_shape=(jax.ShapeDtypeStruct((t, d_ssd), BF16), jax.ShapeDtypeStruct((t, d_ssd), BF16)),
        grid=(batch, nc),
        in_specs=specs(fwd) + specs(bwd) + [
            pl.BlockSpec((1, LANE), const2),
            pl.BlockSpec((1, LANE), const2),
            pl.BlockSpec((2, LANE, d_ssd), lambda b, c: (0, 0, 0)),
        ],
        out_specs=(
            pl.BlockSpec((CHUNK, d_ssd), lambda b, c: (fwd(b, c), 0)),
            pl.BlockSpec((CHUNK, d_ssd), lambda b, c: (bwd(b, c), 0)),
        ),
        scratch_shapes=[pltpu.VMEM((SSD_GROUPS, D_STATE, gw), F32),
                        pltpu.VMEM((SSD_GROUPS, D_STATE, gw), F32)],
        compiler_params=_cparams(("parallel", "arbitrary")),
        name="ssd_scan",
    )(xbc, xbc, xbc, dt_raw, xbc, xbc, xbc, dt_raw, bias, alog, expand)


def _gelu(x):
    return 0.5 * x * (1.0 + lax.erf(x * (2.0 ** -0.5)))


def _mixer_out_kernel(z_ref, xs_ref, yf_ref, yb_ref, u_ref, v_ref, dskip_ref, sg_ref, gg_ref,
                      ws_ref, bs_ref, o_ref, *, d_ssd):
    gw = d_ssd // SSD_GROUPS
    y = (yf_ref[...].astype(F32) + yb_ref[...].astype(F32)
         + xs_ref[...].astype(F32) * dskip_ref[...]) * _silu(z_ref[...].astype(F32))
    for g in range(SSD_GROUPS):
        sl = slice(g * gw, (g + 1) * gw)
        o_ref[:, sl] = _rms_rows(y[:, sl], sg_ref[:, sl]).astype(o_ref.dtype)

    uu = _gelu(u_ref[...].astype(F32))
    vn = _rms_rows(_gelu(v_ref[...].astype(F32)), gg_ref[...]).astype(BF16)
    bs = bs_ref[...]
    for g in range(ws_ref.shape[0]):
        sl = slice(g * GMLP_GROUP_WIDTH, (g + 1) * GMLP_GROUP_WIDTH)
        sp = jnp.dot(ws_ref[g], vn[:, sl], preferred_element_type=F32) + bs[:, g:g + 1]
        o_ref[:, d_ssd + g * GMLP_GROUP_WIDTH:d_ssd + (g + 1) * GMLP_GROUP_WIDTH] = (
            uu[:, sl] * sp).astype(o_ref.dtype)


def mixer_out(proj, proj_uv, xbc, y_f, y_b, d_skip, ssd_norm_g, gmlp_norm_g, gmlp_ws, gmlp_bs, *, d_ssd,
              d_gmlp):
    t = proj.shape[0]
    assert d_ssd == d_gmlp
    w = d_ssd
    n_groups = gmlp_ws.shape[0]
    dskip = jnp.repeat(d_skip.astype(F32), SSD_HEAD_DIM).reshape(1, d_ssd)
    row = lambda i: (i, 0)
    const = lambda i: (0, 0)
    return pl.pallas_call(
        functools.partial(_mixer_out_kernel, d_ssd=d_ssd),
        out_shape=jax.ShapeDtypeStruct((t, d_ssd + d_gmlp), BF16),
        grid=(t // CHUNK,),
        in_specs=[
            pl.BlockSpec((CHUNK, w), row),
            pl.BlockSpec((CHUNK, w), row),
            pl.BlockSpec((CHUNK, w), row),
            pl.BlockSpec((CHUNK, w), row),
            pl.BlockSpec((CHUNK, w), row),
            pl.BlockSpec((CHUNK, w), lambda i: (i, 1)),
            pl.BlockSpec((1, w), const),
            pl.BlockSpec((1, w), const),
            pl.BlockSpec((1, w), const),
            pl.BlockSpec((n_groups, CHUNK, CHUNK), lambda i: (0, 0, 0)),
            pl.BlockSpec((CHUNK, n_groups), const),
        ],
        out_specs=pl.BlockSpec((CHUNK, d_ssd + d_gmlp), row),
        compiler_params=_cparams(("parallel",)),
        name="mixer_out",
    )(proj, xbc, y_f, y_b, proj_uv, proj_uv, dskip, ssd_norm_g.reshape(1, -1).astype(F32),
      gmlp_norm_g.reshape(1, -1).astype(F32), gmlp_ws.astype(BF16), gmlp_bs.T.astype(F32))


def _xattn_kernel(q_ref, k_ref, v_ref, o_ref, *, scale):
    s = lax.dot_general(q_ref[...], k_ref[...], (((1,), (1,)), ((), ())),
                        preferred_element_type=F32) * scale
    p = jnp.exp(s - jnp.max(s, axis=-1, keepdims=True))
    denom = jnp.sum(p, axis=-1, keepdims=True)
    o = jnp.dot(p.astype(BF16), v_ref[...], preferred_element_type=F32)
    o_ref[...] = (o / denom).astype(o_ref.dtype)


def cross_attention(q, kv, *, batch, seq, mem_len, tq=1024):
    t, d = q.shape
    hd = d // XATTN_HEADS
    nq = seq // tq
    return pl.pallas_call(
        functools.partial(_xattn_kernel, scale=hd ** -0.5),
        out_shape=jax.ShapeDtypeStruct((t, d), BF16),
        grid=(batch, XATTN_HEADS, nq),
        in_specs=[
            pl.BlockSpec((tq, hd), lambda b, h, i: (b * nq + i, h)),
            pl.BlockSpec((mem_len, hd), lambda b, h, i: (b, h)),
            pl.BlockSpec((mem_len, hd), lambda b, h, i: (b, XATTN_HEADS + h)),
        ],
        out_specs=pl.BlockSpec((tq, hd), lambda b, h, i: (b * nq + i, h)),
        compiler_params=_cparams(("parallel", "parallel", "parallel")),
        name="cross_attention",
    )(q, kv, kv)


def _router_kernel(x_ref, g_ref, whi_ref, wlo_ref, afft_ref, affn_ref, *, n_experts):
    h = _rms_rows(x_ref[...], g_ref[...])
    h_hi = h.astype(BF16)
    h_lo = (h - h_hi.astype(F32)).astype(BF16)
    logits = (jnp.dot(h_hi, whi_ref[...], preferred_element_type=F32)
              + jnp.dot(h_lo, whi_ref[...], preferred_element_type=F32)
              + jnp.dot(h_hi, wlo_ref[...], preferred_element_type=F32))
    lane = lax.broadcasted_iota(jnp.int32, logits.shape, 1)
    valid = lane < n_experts
    lm = jnp.where(valid, logits, -jnp.inf)
    q = jnp.where(valid, jnp.exp(lm - jnp.max(lm, axis=1, keepdims=True)), 0.0)
    aff = q / jnp.sum(q, axis=1, keepdims=True)
    affn_ref[...] = aff
    afft_ref[...] = aff.T[0:n_experts, :]


def router_affinity(x, g, w_router, layer, *, batch, seq, tm=512):
    t, d = x.shape
    e = w_router.shape[2]
    w_pad = jnp.zeros((d, LANE), F32).at[:, :e].set(w_router[layer].astype(F32))
    w_hi = w_pad.astype(BF16)
    w_lo = (w_pad - w_hi.astype(F32)).astype(BF16)
    ns = seq // tm
    return pl.pallas_call(
        functools.partial(_router_kernel, n_experts=e),
        out_shape=(jax.ShapeDtypeStruct((batch, e, seq), F32), jax.ShapeDtypeStruct((t, LANE), F32)),
        grid=(batch, ns),
        in_specs=[
            pl.BlockSpec((tm, d), lambda b, i: (b * ns + i, 0)),
            pl.BlockSpec((1, d), lambda b, i: (0, 0)),
            pl.BlockSpec((d, LANE), lambda b, i: (0, 0)),
            pl.BlockSpec((d, LANE), lambda b, i: (0, 0)),
        ],
        out_specs=(pl.BlockSpec((None, e, tm), lambda b, i: (b, 0, i)),
                   pl.BlockSpec((tm, LANE), lambda b, i: (b * ns + i, 0))),
        compiler_params=_cparams(("parallel", "parallel")),
        name="router",
    )(x, g.reshape(1, d).astype(F32), w_hi, w_lo)


BISECT_ITERS = 152


def _route_kernel(afft_ref, affn_ref, idx_ref, gate_ref, slot_ref, *, cap):
    n_exp, seq = afft_ref.shape
    blk = LANE
    nblk = seq // blk
    capf = float(cap)

    xt = afft_ref[...]

    def bisect(_, carry):
        lo, hi = carry
        mid = 0.5 * (lo + hi)
        cnt = jnp.sum(jnp.where(xt >= mid, 1.0, 0.0), axis=1, keepdims=True)
        ge = cnt >= capf
        return jnp.where(ge, mid, lo), jnp.where(ge, hi, mid)

    lo, hi = lax.fori_loop(0, BISECT_ITERS, bisect,
                           (jnp.zeros((n_exp, 1), F32), jnp.full((n_exp, 1), 2.0, F32)))
    r_i = lax.broadcasted_iota(jnp.int32, (n_exp, LANE), 0)
    c_i = lax.broadcasted_iota(jnp.int32, (n_exp, LANE), 1)
    diag = r_i == c_i
    lane_ok = lax.broadcasted_iota(jnp.int32, (1, LANE), 1) < n_exp
    lo_r = jnp.where(lane_ok, jnp.sum(jnp.where(diag, lo, 0.0), axis=0, keepdims=True), 4.0)
    hi_r = jnp.where(lane_ok, jnp.sum(jnp.where(diag, hi, 0.0), axis=0, keepdims=True), 4.0)

    n_gt = jnp.sum(jnp.where(affn_ref[...] >= hi_r, 1.0, 0.0), axis=0, keepdims=True)
    need = capf - n_gt

    tr = lax.broadcasted_iota(jnp.int32, (blk, blk), 0)
    tc = lax.broadcasted_iota(jnp.int32, (blk, blk), 1)
    tril = jnp.where(tr >= tc, 1.0, 0.0).astype(BF16)
    carry_eq = jnp.zeros((1, LANE), F32)
    carry_sel = jnp.zeros((1, LANE), F32)
    for k in range(nblk):
        x = affn_ref[k * blk:(k + 1) * blk, :]
        gt = x >= hi_r
        eq = jnp.logical_and(x >= lo_r, jnp.logical_not(gt))
        pos_eq = jnp.dot(tril, jnp.where(eq, 1.0, 0.0).astype(BF16), preferred_element_type=F32) + carry_eq
        carry_eq = pos_eq[blk - 1:blk, :]
        sel = jnp.logical_or(gt, jnp.logical_and(eq, pos_eq <= need))
        cs = jnp.dot(tril, jnp.where(sel, 1.0, 0.0).astype(BF16), preferred_element_type=F32) + carry_sel
        carry_sel = cs[blk - 1:blk, :]
        slot_ref[k * blk:(k + 1) * blk, :] = jnp.where(sel, cs - 1.0, -1.0)

    s_iota = lax.broadcasted_iota(jnp.int32, (blk, cap), 1).astype(F32)
    t_iota = lax.broadcasted_iota(jnp.int32, (blk, 1), 0).astype(F32)
    for e in range(n_exp):
        def body(k, acc):
            acc_i, acc_g = acc
            r0 = pl.multiple_of(k * blk, blk)
            slot = slot_ref[pl.ds(r0, blk), :][:, e:e + 1]
            aff = affn_ref[pl.ds(r0, blk), :][:, e:e + 1]
            hit = jnp.broadcast_to(slot, (blk, cap)) == s_iota
            tok = t_iota + lax.convert_element_type(k * blk, F32)
            acc_i = acc_i + jnp.sum(jnp.where(hit, tok, 0.0), axis=0, keepdims=True)
            acc_g = acc_g + jnp.sum(jnp.where(hit, aff, 0.0), axis=0, keepdims=True)
            return acc_i, acc_g

        acc_i, acc_g = lax.fori_loop(0, nblk, body, (jnp.zeros((1, cap), F32), jnp.zeros((1, cap), F32)))
        idx_ref[e:e + 1, :] = acc_i.astype(jnp.int32)
        gate_ref[e:e + 1, :] = acc_g


def route(aff_t, aff_n, *, cap):
    batch, e, seq = aff_t.shape
    return pl.pallas_call(
        functools.partial(_route_kernel, cap=cap),
        out_shape=(jax.ShapeDtypeStruct((batch, e, cap), jnp.int32),
                   jax.ShapeDtypeStruct((batch, e, cap), F32)),
        grid=(batch,),
        in_specs=[pl.BlockSpec((None, e, seq), lambda b: (b, 0, 0)),
                  pl.BlockSpec((seq, LANE), lambda b: (b, 0))],
        out_specs=(pl.BlockSpec((None, e, cap), lambda b: (b, 0, 0)),
                   pl.BlockSpec((None, e, cap), lambda b: (b, 0, 0))),
        scratch_shapes=[pltpu.VMEM((seq, LANE), F32)],
        compiler_params=_cparams(("parallel",)),
        name="route",
    )(aff_t, aff_n)


def _row_copy(src_ref, src_row, dst_ref, dst_row, sem):
    return pltpu.make_async_copy(src_ref.at[pl.ds(src_row, 1)], dst_ref.at[pl.ds(dst_row, 1)], sem)


ROW_UNROLL = 8
NORM_ROWS = 128


def _expert_up_kernel(idx_ref, x_hbm, g_ref, wg_ref, wu_ref, o_ref, stage_ref, xn_ref, sem):
    e = pl.program_id(0)
    n_exp = pl.num_programs(0)
    rows = xn_ref.shape[0]

    def start_gather(expert):
        def body(j, c):
            for p in range(ROW_UNROLL):
                r = ROW_UNROLL * j + p
                _row_copy(x_hbm, idx_ref[expert * rows + r], stage_ref, r, sem).start(priority=p % 2)
            return c

        lax.fori_loop(0, rows // ROW_UNROLL, body, 0)

    @pl.when(pl.program_id(1) == 0)
    def _():
        @pl.when(e == 0)
        def _():
            start_gather(0)

        pltpu.make_async_copy(x_hbm.at[pl.ds(0, rows)], stage_ref, sem).wait()

        def norm(i, c):
            r = pl.multiple_of(i * NORM_ROWS, NORM_ROWS)
            xn_ref[pl.ds(r, NORM_ROWS), :] = _rms_rows(
                stage_ref[pl.ds(r, NORM_ROWS), :], g_ref[...]).astype(xn_ref.dtype)
            return c

        lax.fori_loop(0, rows // NORM_ROWS, norm, 0)

        @pl.when(e + 1 < n_exp)
        def _():
            start_gather(e + 1)

    xn = xn_ref[...]
    gate = jnp.dot(xn, wg_ref[...].astype(BF16), preferred_element_type=F32)
    up = jnp.dot(xn, wu_ref[...].astype(BF16), preferred_element_type=F32)
    o_ref[...] = (_silu(gate) * up).astype(o_ref.dtype)


def expert_up(idx_flat, x, g, w_gate_up, layer, *, rows, d_expert, tf=256):
    t, d = x.shape
    e = w_gate_up.shape[1]
    nf = d_expert // tf
    grid_spec = pltpu.PrefetchScalarGridSpec(
        num_scalar_prefetch=1,
        grid=(e, nf),
        in_specs=[
            pl.BlockSpec(memory_space=pl.ANY),
            pl.BlockSpec((1, d), lambda i, f, idx: (0, 0)),
            pl.BlockSpec((None, None, d, tf), lambda i, f, idx: (layer, i, 0, f)),
            pl.BlockSpec((None, None, d, tf), lambda i, f, idx: (layer, i, 0, nf + f)),
        ],
        out_specs=pl.BlockSpec((None, rows, tf), lambda i, f, idx: (i, 0, f)),
        scratch_shapes=[
            pltpu.VMEM((rows, d), F32),
            pltpu.VMEM((rows, d), BF16),
            pltpu.SemaphoreType.DMA(()),
        ],
    )
    return pl.pallas_call(
        _expert_up_kernel,
        out_shape=jax.ShapeDtypeStruct((e, rows, d_expert), BF16),
        grid_spec=grid_spec,
        compiler_params=_cparams(("arbitrary", "arbitrary")),
        name="expert_up",
    )(idx_flat, x, g.reshape(1, d).astype(F32), w_gate_up, w_gate_up)


def _expert_down_kernel(act_ref, w_ref, gate_ref, o_ref):
    y = jnp.dot(act_ref[...], w_ref[...].astype(BF16), preferred_element_type=F32) * gate_ref[...]
    o_ref[...] = y.reshape(o_ref.shape)


def expert_down(act, w_down, gates, layer, *, tn=1024):
    e, rows, f = act.shape
    d = w_down.shape[3]
    assert tn == SUBLANE * LANE
    return pl.pallas_call(
        _expert_down_kernel,
        out_shape=jax.ShapeDtypeStruct((e, rows, d // LANE, LANE), F32),
        grid=(e, d // tn),
        in_specs=[
            pl.BlockSpec((None, rows, f), lambda i, j: (i, 0, 0)),
            pl.BlockSpec((None, None, f, tn), lambda i, j: (layer, i, 0, j)),
            pl.BlockSpec((None, rows, 1), lambda i, j: (i, 0, 0)),
        ],
        out_specs=pl.BlockSpec((None, rows, SUBLANE, LANE), lambda i, j: (i, 0, j, 0)),
        compiler_params=_cparams(("parallel", "arbitrary")),
        name="expert_down",
    )(act, w_down, gates)


COMBINE_CHUNK = 512
COMBINE_UNROLL = 16


def _combine_kernel(idx_ref, y3_ref, x_hbm, o_hbm, acc_ref, stage_ref, sem_in, sem_out,
                    *, seq, cap, n_batch):
    n = pl.program_id(0)
    b = pl.program_id(1)
    e = pl.program_id(2)
    n_exp = pl.num_programs(2)
    cn = stage_ref.shape[2]
    ch = stage_ref.shape[1]
    n_chunks = seq // ch
    col = pl.multiple_of(n * cn, cn)

    def x_copy(c):
        row = pl.multiple_of(b * seq + c * ch, ch)
        return pltpu.make_async_copy(x_hbm.at[pl.ds(row, ch), pl.ds(col, cn)], stage_ref.at[c % 2],
                                     sem_in.at[c % 2])

    def o_copy(c):
        row = pl.multiple_of(b * seq + c * ch, ch)
        return pltpu.make_async_copy(stage_ref.at[c % 2], o_hbm.at[pl.ds(row, ch), pl.ds(col, cn)],
                                     sem_out.at[c % 2])

    @pl.when(e == 0)
    def _():
        x_copy(0).start()
        for c in range(n_chunks):
            x_copy(c).wait()
            if c + 1 < n_chunks:
                x_copy(c + 1).start()
            acc_ref[c * ch:(c + 1) * ch] = stage_ref[c % 2].reshape(ch, SUBLANE, LANE)

    base = (e * n_batch + b) * cap

    def body(i, c):
        toks = [idx_ref[base + COMBINE_UNROLL * i + u] for u in range(COMBINE_UNROLL)]
        vals = [acc_ref[toks[u]] + y3_ref[COMBINE_UNROLL * i + u] for u in range(COMBINE_UNROLL)]
        for u in range(COMBINE_UNROLL):
            acc_ref[toks[u]] = vals[u]
        return c

    lax.fori_loop(0, cap // COMBINE_UNROLL, body, 0)

    @pl.when(e == n_exp - 1)
    def _():
        for c in range(n_chunks):
            if c >= 2:
                o_copy(c - 2).wait()
            stage_ref[c % 2] = acc_ref[c * ch:(c + 1) * ch].reshape(ch, cn)
            o_copy(c).start()
        for c in range(max(n_chunks - 2, 0), n_chunks):
            o_copy(c).wait()


def moe_combine(idx_local, y, x, *, batch, seq, cap):
    e, rows = y.shape[:2]
    t, d = x.shape
    cn = SUBLANE * LANE
    grid_spec = pltpu.PrefetchScalarGridSpec(
        num_scalar_prefetch=1,
        grid=(d // cn, batch, e),
        in_specs=[
            pl.BlockSpec((None, cap, SUBLANE, LANE), lambda n, b, i, idx: (i, b, n, 0)),
            pl.BlockSpec(memory_space=pl.ANY),
        ],
        out_specs=pl.BlockSpec(memory_space=pl.ANY),
        scratch_shapes=[
            pltpu.VMEM((seq, SUBLANE, LANE), F32),
            pltpu.VMEM((2, min(COMBINE_CHUNK, seq), cn), F32),
            pltpu.SemaphoreType.DMA((2,)),
            pltpu.SemaphoreType.DMA((2,)),
        ],
    )
    return pl.pallas_call(
        functools.partial(_combine_kernel, seq=seq, cap=cap, n_batch=batch),
        out_shape=jax.ShapeDtypeStruct((t, d), F32),
        grid_spec=grid_spec,
        compiler_params=_cparams(("arbitrary", "arbitrary", "arbitrary")),
        name="moe_combine",
    )(idx_local, y, x)


def _final_norm_kernel(x_ref, g_ref, o_ref):
    o_ref[...] = _rms_rows(x_ref[...], g_ref[...])


def final_norm(x, g, *, tm=512):
    t, d = x.shape
    return pl.pallas_call(
        _final_norm_kernel,
        out_shape=jax.ShapeDtypeStruct((t, d), F32),
        grid=(t // tm,),
        in_specs=[pl.BlockSpec((tm, d), lambda i: (i, 0)), pl.BlockSpec((1, d), lambda i: (0, 0))],
        out_specs=pl.BlockSpec((tm, d), lambda i: (i, 0)),
        compiler_params=_cparams(("parallel",)),
        name="final_norm",
    )(x, g.reshape(1, d).astype(F32))


def kernel(x, mem, norm_mix_g, w_in, conv_w, conv_b, dt_bias, a_log, d_skip, ssd_norm_g, gmlp_norm_g,
           gmlp_ws, gmlp_bs, w_out, norm_xattn_g, norm_mem_g, w_q, w_kv, w_o, norm_moe_g, w_router,
           w_gate_up, w_down, final_norm_g):
    batch, seq, d_model = x.shape
    mem_len = mem.shape[1]
    depth = w_in.shape[0]
    n_heads = dt_bias.shape[2]
    d_ssd = n_heads * SSD_HEAD_DIM
    conv_ch = conv_w.shape[2]
    d_gmlp = gmlp_norm_g.shape[1]
    d_expert = w_down.shape[2]
    n_experts = w_router.shape[2]
    cap = CAPACITY_FACTOR * seq // n_experts
    t = batch * seq
    o_dt = d_ssd + conv_ch
    o_u = o_dt + 2 * n_heads

    xf = x.reshape(t, d_model)
    memf = mem.reshape(batch * mem_len, d_model)
    w_in_t = jnp.swapaxes(w_in, 1, 2).reshape(depth * w_in.shape[2], d_model)
    for l in range(depth):
        h = norm_cast(xf, norm_mix_g[l])
        r0 = l * w_in.shape[2]
        proj = matmul_nt(h, w_in_t, r0, o_dt)
        proj_uv = matmul_nt(h, w_in_t, r0 + o_u, 2 * d_gmlp)
        dt_raw = matmul_nt(h, w_in_t, r0 + o_dt, LANE, out_dtype=F32)
        xbc = conv_silu(proj, conv_w[l], conv_b[l], batch=batch, seq=seq, col0=d_ssd)
        y_f, y_b = ssd_scan(xbc, dt_raw, dt_bias[l], a_log[l], batch=batch, seq=seq, d_ssd=d_ssd)
        y_cat = mixer_out(proj, proj_uv, xbc, y_f, y_b, d_skip[l], ssd_norm_g[l], gmlp_norm_g[l],
                          gmlp_ws[l], gmlp_bs[l], d_ssd=d_ssd, d_gmlp=d_gmlp)
        xf, xg, ss = matmul(y_cat, w_out, l, res=xf, next_gain=norm_xattn_g[l])

        q = matmul(xg, w_q, l, row_ss=ss)
        kv = matmul(norm_cast(memf, norm_mem_g[l]), w_kv, l)
        o = cross_attention(q, kv, batch=batch, seq=seq, mem_len=mem_len)
        xf = matmul(o, w_o, l, res=xf)

        aff_t, aff_n = router_affinity(xf, norm_moe_g[l], w_router, l, batch=batch, seq=seq)
        idx, gate = route(aff_t, aff_n, cap=cap)
        idx_local = jnp.swapaxes(idx, 0, 1)
        idx_flat = (idx_local + (jnp.arange(batch, dtype=jnp.int32) * seq)[None, :, None]).reshape(-1)
        gates = jnp.swapaxes(gate, 0, 1).reshape(n_experts, batch * cap, 1)
        act = expert_up(idx_flat, xf, norm_moe_g[l], w_gate_up, l, rows=batch * cap, d_expert=d_expert)
        y = expert_down(act, w_down, gates, l)
        xf = moe_combine(idx_local.reshape(-1), y, xf, batch=batch, seq=seq, cap=cap)
    out = final_norm(xf, final_norm_g)
    return out.reshape(batch, seq, d_model)
```

```python
import functools

import jax
import jax.numpy as jnp
from jax import lax
from jax.experimental import pallas as pl
from jax.experimental.pallas import tpu as pltpu

F32 = jnp.float32
BF16 = jnp.bfloat16

RMS_EPS = 1e-6
SSD_HEAD_DIM = 64
SSD_GROUPS = 8
D_STATE = 128
D_CONV = 5
CHUNK = 128
GMLP_GROUP_WIDTH = 128
XATTN_HEADS = 4
N_EXPERTS = 16
CAPACITY_FACTOR = 2
LANE = 128
SUBLANE = 8
VMEM_LIMIT = 56 * 1024 * 1024


def _cparams(sem, vmem=VMEM_LIMIT):
    return pltpu.CompilerParams(dimension_semantics=sem, vmem_limit_bytes=vmem)


def _silu(x):
    return x * (1.0 / (1.0 + jnp.exp(-x)))


def _rms_rows(x, g):
    ms = jnp.mean(x * x, axis=-1, keepdims=True)
    return x * lax.rsqrt(ms + RMS_EPS) * g


def _norm_cast_kernel(x_ref, g_ref, o_ref):
    o_ref[...] = _rms_rows(x_ref[...], g_ref[...]).astype(o_ref.dtype)


def norm_cast(x, g, *, tm=512):
    m, k = x.shape
    tm = min(tm, m)
    return pl.pallas_call(
        _norm_cast_kernel,
        out_shape=jax.ShapeDtypeStruct((m, k), BF16),
        grid=(m // tm,),
        in_specs=[pl.BlockSpec((tm, k), lambda i: (i, 0)), pl.BlockSpec((1, k), lambda i: (0, 0))],
        out_specs=pl.BlockSpec((tm, k), lambda i: (i, 0)),
        compiler_params=_cparams(("parallel",)),
        name="norm_cast",
    )(x, g.reshape(1, k).astype(F32))


def _mm_kernel(a_ref, w_ref, o_ref):
    o_ref[...] = jnp.dot(a_ref[...], w_ref[...].astype(BF16),
                         preferred_element_type=F32).astype(o_ref.dtype)


def _mm_nt_kernel(a_ref, wt_ref, o_ref):
    o_ref[...] = lax.dot_general(a_ref[...], wt_ref[...].astype(BF16), (((1,), (1,)), ((), ())),
                                 preferred_element_type=F32).astype(o_ref.dtype)


def matmul_nt(a, wt, row0, n, *, out_dtype=BF16, tm=1024, tn=512):
    m, k = a.shape
    tm = min(tm, m)
    tn = min(tn, n)
    assert m % tm == 0 and n % tn == 0 and row0 % 8 == 0 and wt.shape[1] == k
    return pl.pallas_call(
        _mm_nt_kernel,
        out_shape=jax.ShapeDtypeStruct((m, n), out_dtype),
        grid=(m // tm, n // tn),
        in_specs=[pl.BlockSpec((tm, k), lambda i, j: (i, 0)),
                  pl.BlockSpec((pl.Element(tn), pl.Element(k)),
                               lambda i, j: (pl.multiple_of(row0 + j * tn, 8), 0))],
        out_specs=pl.BlockSpec((tm, tn), lambda i, j: (i, j)),
        compiler_params=_cparams(("parallel", "arbitrary")),
        name="matmul_nt",
    )(a, wt)


def _mm_res_kernel(a_ref, w_ref, r_ref, o_ref):
    o_ref[...] = r_ref[...] + jnp.dot(a_ref[...], w_ref[...].astype(BF16), preferred_element_type=F32)


def _mm_res_norm_kernel(a_ref, w_ref, r_ref, g_ref, o_ref, xg_ref, ss_ref):
    x = r_ref[...] + jnp.dot(a_ref[...], w_ref[...].astype(BF16), preferred_element_type=F32)
    o_ref[...] = x
    xg_ref[...] = (x * g_ref[...]).astype(xg_ref.dtype)

    @pl.when(pl.program_id(1) == 0)
    def _():
        ss_ref[...] = jnp.zeros_like(ss_ref)

    ss_ref[...] += jnp.sum(x * x, axis=1, keepdims=True)


def _mm_rowscale_kernel(a_ref, w_ref, ss_ref, o_ref):
    inv = lax.rsqrt(ss_ref[...] * (1.0 / a_ref.shape[1]) + RMS_EPS)
    o_ref[...] = (jnp.dot(a_ref[...], w_ref[...].astype(BF16), preferred_element_type=F32)
                  * inv).astype(o_ref.dtype)


def matmul(a, w, layer, *, n_cols=None, res=None, out_dtype=BF16, tm=1024, tn=512, next_gain=None,
           row_ss=None):
    m, k = a.shape
    n = w.shape[2] if n_cols is None else n_cols
    tm = min(tm, m)
    tn = min(tn, n)
    assert m % tm == 0 and n % tn == 0
    a_spec = pl.BlockSpec((tm, k), lambda i, j: (i, 0))
    w_spec = pl.BlockSpec((None, k, tn), lambda i, j: (layer, 0, j))
    o_spec = pl.BlockSpec((tm, tn), lambda i, j: (i, j))
    if next_gain is not None:
        return pl.pallas_call(
            _mm_res_norm_kernel,
            out_shape=(jax.ShapeDtypeStruct((m, n), F32), jax.ShapeDtypeStruct((m, n), BF16),
                       jax.ShapeDtypeStruct((m, 1), F32)),
            grid=(m // tm, n // tn),
            in_specs=[a_spec, w_spec, o_spec, pl.BlockSpec((1, tn), lambda i, j: (0, j))],
            out_specs=(o_spec, o_spec, pl.BlockSpec((tm, 1), lambda i, j: (i, 0))),
            compiler_params=_cparams(("parallel", "arbitrary")),
            name="matmul_residual_norm",
        )(a, w, res, next_gain.reshape(1, n).astype(F32))
    if row_ss is not None:
        return pl.pallas_call(
            _mm_rowscale_kernel,
            out_shape=jax.ShapeDtypeStruct((m, n), out_dtype),
            grid=(m // tm, n // tn),
            in_specs=[a_spec, w_spec, pl.BlockSpec((tm, 1), lambda i, j: (i, 0))],
            out_specs=o_spec,
            compiler_params=_cparams(("parallel", "arbitrary")),
            name="matmul_rowscale",
        )(a, w, row_ss)
    if res is None:
        return pl.pallas_call(
            _mm_kernel,
            out_shape=jax.ShapeDtypeStruct((m, n), out_dtype),
            grid=(m // tm, n // tn),
            in_specs=[a_spec, w_spec],
            out_specs=o_spec,
            compiler_params=_cparams(("parallel", "arbitrary")),
            name="matmul",
        )(a, w)
    return pl.pallas_call(
        _mm_res_kernel,
        out_shape=jax.ShapeDtypeStruct((m, n), F32),
        grid=(m // tm, n // tn),
        in_specs=[a_spec, w_spec, o_spec],
        out_specs=o_spec,
        compiler_params=_cparams(("parallel", "arbitrary")),
        name="matmul_residual",
    )(a, w, res)


HALO = 16


def _conv_kernel(prev_ref, cur_ref, next_ref, w_ref, b_ref, o_ref, ext_ref, *, n_seq_tiles):
    ts = cur_ref.shape[0]
    s = pl.program_id(1)
    prev = prev_ref[...].astype(F32)
    nxt = next_ref[...].astype(F32)
    ext_ref[0:HALO, :] = jnp.where(s == 0, 0.0, prev)
    ext_ref[HALO:HALO + ts, :] = cur_ref[...].astype(F32)
    ext_ref[HALO + ts:, :] = jnp.where(s == n_seq_tiles - 1, 0.0, nxt)
    acc = jnp.zeros(cur_ref.shape, F32) + b_ref[...]
    for k in range(D_CONV):
        off = HALO - D_CONV // 2 + k
        acc = acc + ext_ref[off:off + ts, :] * w_ref[k:k + 1, :]
    o_ref[...] = _silu(acc).astype(o_ref.dtype)


def conv_silu(proj, conv_w, conv_b, *, batch, seq, col0, ts=512, tc=1024):
    c = conv_w.shape[1]
    t = batch * seq
    nst = seq // ts
    hb = ts // HALO
    cb0 = col0 // tc
    assert col0 % tc == 0 and c % tc == 0 and seq % ts == 0
    last_halo = t // HALO - 1

    def prev_map(b, s, j):
        return (jnp.maximum((b * nst + s) * hb - 1, 0), cb0 + j)

    def next_map(b, s, j):
        return (jnp.minimum((b * nst + s + 1) * hb, last_halo), cb0 + j)

    return pl.pallas_call(
        functools.partial(_conv_kernel, n_seq_tiles=nst),
        out_shape=jax.ShapeDtypeStruct((t, c), BF16),
        grid=(batch, nst, c // tc),
        in_specs=[
            pl.BlockSpec((HALO, tc), prev_map),
            pl.BlockSpec((ts, tc), lambda b, s, j: (b * nst + s, cb0 + j)),
            pl.BlockSpec((HALO, tc), next_map),
            pl.BlockSpec((D_CONV, tc), lambda b, s, j: (0, j)),
            pl.BlockSpec((1, tc), lambda b, s, j: (0, j)),
        ],
        out_specs=pl.BlockSpec((ts, tc), lambda b, s, j: (b * nst + s, j)),
        scratch_shapes=[pltpu.VMEM((ts + 2 * HALO, tc), F32)],
        compiler_params=_cparams(("parallel", "parallel", "parallel")),
        name="conv_silu",
    )(proj, proj, proj, conv_w.astype(F32), conv_b.reshape(1, c).astype(F32))


def _ssd_direction(xs_ref, b_ref, c_ref, dt_ref, bias_ref, alog_ref, expand_ref, st_ref, y_ref,
                   *, reverse, col0, n_heads):
    L = CHUNK
    hp = SSD_HEAD_DIM
    heads_per_group = n_heads // SSD_GROUPS
    gw = heads_per_group * hp
    row = lax.broadcasted_iota(jnp.int32, (L, L), 0)
    col = lax.broadcasted_iota(jnp.int32, (L, L), 1)
    keep = (col >= row) if reverse else (col <= row)
    tri = jnp.where(keep, 1.0, 0.0).astype(F32)

    xraw = dt_ref[...] + bias_ref[...]
    dt = jnp.maximum(xraw, 0.0) + jnp.log1p(jnp.exp(-jnp.abs(xraw)))
    a = -jnp.exp(alog_ref[...])
    dta = dt * a
    acs = jnp.dot(tri, dta, preferred_element_type=F32, precision=lax.Precision.HIGHEST)
    acs_t = acs.T
    total = acs[0:1, :] if reverse else acs[L - 1:L, :]
    dte = jnp.exp(total - acs)
    ea = jnp.exp(acs)
    cdec = jnp.broadcast_to(jnp.exp(total), (8, LANE))

    stack = jnp.concatenate([dt, dte, ea, cdec], axis=0).astype(BF16)
    ex = jnp.dot(stack, expand_ref[...], preferred_element_type=F32)
    dt_x = ex[0:L]
    dte_x = ex[L:2 * L]
    ea_x = ex[2 * L:3 * L]
    cdec_x = ex[3 * L:3 * L + 1]

    lane_head = lax.broadcasted_iota(jnp.int32, (L, gw), 1) // hp
    for g in range(SSD_GROUPS):
        sl = slice(g * gw, (g + 1) * gw)
        xs_g = xs_ref[:, sl].astype(F32)
        xdt = xs_g * dt_x[:, sl]
        xdt_bf = xdt.astype(BF16)
        xdtw_bf = (xdt * dte_x[:, sl]).astype(BF16)
        b_g = b_ref[:, g * D_STATE:(g + 1) * D_STATE]
        c_g = c_ref[:, g * D_STATE:(g + 1) * D_STATE]
        cb = lax.dot_general(c_g, b_g, (((1,), (1,)), ((), ())), preferred_element_type=F32)
        y_g = jnp.zeros((L, gw), F32)
        for r in range(heads_per_group):
            hc = col0 + g * heads_per_group + r
            seg = acs[:, hc:hc + 1] - acs_t[hc:hc + 1, :]
            decay = jnp.exp(jnp.where(keep, seg, -jnp.inf))
            m = (cb * decay).astype(BF16)
            x_r = jnp.where(lane_head == r, xdt_bf, jnp.zeros_like(xdt_bf))
            y_g = y_g + jnp.dot(m, x_r, preferred_element_type=F32)
        st = st_ref[g]
        y_off = jnp.dot(c_g, st.astype(BF16), preferred_element_type=F32) * ea_x[:, sl]
        y_ref[:, sl] = (y_g + y_off).astype(y_ref.dtype)
        b_t = b_g.astype(F32).T.astype(BF16)
        st_ref[g] = st * cdec_x[:, sl] + jnp.dot(b_t, xdtw_bf, preferred_element_type=F32)


def _ssd_kernel(xs_f, b_f, c_f, dt_f, xs_b, b_b, c_b, dt_b, bias_ref, alog_ref, expand_ref,
                yf_ref, yb_ref, stf_ref, stb_ref, *, n_heads):
    @pl.when(pl.program_id(1) == 0)
    def _():
        stf_ref[...] = jnp.zeros_like(stf_ref)
        stb_ref[...] = jnp.zeros_like(stb_ref)

    _ssd_direction(xs_f, b_f, c_f, dt_f, bias_ref, alog_ref, expand_ref.at[0], stf_ref, yf_ref,
                   reverse=False, col0=0, n_heads=n_heads)
    _ssd_direction(xs_b, b_b, c_b, dt_b, bias_ref, alog_ref, expand_ref.at[1], stb_ref, yb_ref,
                   reverse=True, col0=n_heads, n_heads=n_heads)


def ssd_scan(xbc, dt_raw, dt_bias, a_log, *, batch, seq, d_ssd):
    t = batch * seq
    nc = seq // CHUNK
    n_heads = d_ssd // SSD_HEAD_DIM
    gn = SSD_GROUPS * D_STATE
    assert d_ssd % gn == 0 and 2 * n_heads <= LANE
    xb = d_ssd // gn
    bias = jnp.zeros((1, LANE), F32).at[0, :2 * n_heads].set(dt_bias.reshape(-1).astype(F32))
    alog = jnp.zeros((1, LANE), F32).at[0, :2 * n_heads].set(a_log.reshape(-1).astype(F32))
    lane_h = jnp.arange(d_ssd) // SSD_HEAD_DIM
    expand = jnp.stack([
        (jnp.arange(LANE)[:, None] == (d * n_heads + lane_h)[None, :]) for d in range(2)
    ]).astype(BF16)

    def fwd(b, c):
        return b * nc + c

    def bwd(b, c):
        return b * nc + (nc - 1 - c)

    def specs(rmap):
        return [
            pl.BlockSpec((CHUNK, d_ssd), lambda b, c: (rmap(b, c), 0)),
            pl.BlockSpec((CHUNK, gn), lambda b, c: (rmap(b, c), xb)),
            pl.BlockSpec((CHUNK, gn), lambda b, c: (rmap(b, c), xb + 1)),
            pl.BlockSpec((CHUNK, LANE), lambda b, c: (rmap(b, c), 0)),
        ]

    const2 = lambda b, c: (0, 0)
    gw = d_ssd // SSD_GROUPS
    return pl.pallas_call(
        functools.partial(_ssd_kernel, n_heads=n_heads),
        out_shape=(jax.ShapeDtypeStruct((t, d_ssd), BF16), jax.ShapeDtypeStruct((t, d_ssd), BF16)),
        grid=(batch, nc),
        in_specs=specs(fwd) + specs(bwd) + [
            pl.BlockSpec((1, LANE), const2),
            pl.BlockSpec((1, LANE), const2),
            pl.BlockSpec((2, LANE, d_ssd), lambda b, c: (0, 0, 0)),
        ],
        out_specs=(
            pl.BlockSpec((CHUNK, d_ssd), lambda b, c: (fwd(b, c), 0)),
            pl.BlockSpec((CHUNK, d_ssd), lambda b, c: (bwd(b, c), 0)),
        ),
        scratch_shapes=[pltpu.VMEM((SSD_GROUPS, D_STATE, gw), F32),
                        pltpu.VMEM((SSD_GROUPS, D_STATE, gw), F32)],
        compiler_params=_cparams(("parallel", "arbitrary")),
        name="ssd_scan",
    )(xbc, xbc, xbc, dt_raw, xbc, xbc, xbc, dt_raw, bias, alog, expand)


def _gelu(x):
    return 0.5 * x * (1.0 + lax.erf(x * (2.0 ** -0.5)))


def _mixer_out_kernel(z_ref, xs_ref, yf_ref, yb_ref, u_ref, v_ref, dskip_ref, sg_ref, gg_ref,
                      ws_ref, bs_ref, o_ref, *, d_ssd):
    gw = d_ssd // SSD_GROUPS
    y = (yf_ref[...].astype(F32) + yb_ref[...].astype(F32)
         + xs_ref[...].astype(F32) * dskip_ref[...]) * _silu(z_ref[...].astype(F32))
    for g in range(SSD_GROUPS):
        sl = slice(g * gw, (g + 1) * gw)
        o_ref[:, sl] = _rms_rows(y[:, sl], sg_ref[:, sl]).astype(o_ref.dtype)

    uu = _gelu(u_ref[...].astype(F32))
    vn = _rms_rows(_gelu(v_ref[...].astype(F32)), gg_ref[...]).astype(BF16)
    bs = bs_ref[...]
    for g in range(ws_ref.shape[0]):
        sl = slice(g * GMLP_GROUP_WIDTH, (g + 1) * GMLP_GROUP_WIDTH)
        sp = jnp.dot(ws_ref[g], vn[:, sl], preferred_element_type=F32) + bs[:, g:g + 1]
        o_ref[:, d_ssd + g * GMLP_GROUP_WIDTH:d_ssd + (g + 1) * GMLP_GROUP_WIDTH] = (
            uu[:, sl] * sp).astype(o_ref.dtype)


def mixer_out(proj, proj_uv, xbc, y_f, y_b, d_skip, ssd_norm_g, gmlp_norm_g, gmlp_ws, gmlp_bs, *, d_ssd,
              d_gmlp):
    t = proj.shape[0]
    assert d_ssd == d_gmlp
    w = d_ssd
    n_groups = gmlp_ws.shape[0]
    dskip = jnp.repeat(d_skip.astype(F32), SSD_HEAD_DIM).reshape(1, d_ssd)
    row = lambda i: (i, 0)
    const = lambda i: (0, 0)
    return pl.pallas_call(
        functools.partial(_mixer_out_kernel, d_ssd=d_ssd),
        out_shape=jax.ShapeDtypeStruct((t, d_ssd + d_gmlp), BF16),
        grid=(t // CHUNK,),
        in_specs=[
            pl.BlockSpec((CHUNK, w), row),
            pl.BlockSpec((CHUNK, w), row),
            pl.BlockSpec((CHUNK, w), row),
            pl.BlockSpec((CHUNK, w), row),
            pl.BlockSpec((CHUNK, w), row),
            pl.BlockSpec((CHUNK, w), lambda i: (i, 1)),
            pl.BlockSpec((1, w), const),
            pl.BlockSpec((1, w), const),
            pl.BlockSpec((1, w), const),
            pl.BlockSpec((n_groups, CHUNK, CHUNK), lambda i: (0, 0, 0)),
            pl.BlockSpec((CHUNK, n_groups), const),
        ],
        out_specs=pl.BlockSpec((CHUNK, d_ssd + d_gmlp), row),
        compiler_params=_cparams(("parallel",)),
        name="mixer_out",
    )(proj, xbc, y_f, y_b, proj_uv, proj_uv, dskip, ssd_norm_g.reshape(1, -1).astype(F32),
      gmlp_norm_g.reshape(1, -1).astype(F32), gmlp_ws.astype(BF16), gmlp_bs.T.astype(F32))


def _xattn_kernel(q_ref, k_ref, v_ref, o_ref, *, scale):
    s = lax.dot_general(q_ref[...], k_ref[...], (((1,), (1,)), ((), ())),
                        preferred_element_type=F32) * scale
    p = jnp.exp(s - jnp.max(s, axis=-1, keepdims=True))
    denom = jnp.sum(p, axis=-1, keepdims=True)
    o = jnp.dot(p.astype(BF16), v_ref[...], preferred_element_type=F32)
    o_ref[...] = (o / denom).astype(o_ref.dtype)


def cross_attention(q, kv, *, batch, seq, mem_len, tq=1024):
    t, d = q.shape
    hd = d // XATTN_HEADS
    nq = seq // tq
    return pl.pallas_call(
        functools.partial(_xattn_kernel, scale=hd ** -0.5),
        out_shape=jax.ShapeDtypeStruct((t, d), BF16),
        grid=(batch, XATTN_HEADS, nq),
        in_specs=[
            pl.BlockSpec((tq, hd), lambda b, h, i: (b * nq + i, h)),
            pl.BlockSpec((mem_len, hd), lambda b, h, i: (b, h)),
            pl.BlockSpec((mem_len, hd), lambda b, h, i: (b, XATTN_HEADS + h)),
        ],
        out_specs=pl.BlockSpec((tq, hd), lambda b, h, i: (b * nq + i, h)),
        compiler_params=_cparams(("parallel", "parallel", "parallel")),
        name="cross_attention",
    )(q, kv, kv)


def _router_kernel(x_ref, g_ref, whi_ref, wlo_ref, afft_ref, affn_ref, *, n_experts):
    h = _rms_rows(x_ref[...], g_ref[...])
    h_hi = h.astype(BF16)
    h_lo = (h - h_hi.astype(F32)).astype(BF16)
    logits = (jnp.dot(h_hi, whi_ref[...], preferred_element_type=F32)
              + jnp.dot(h_lo, whi_ref[...], preferred_element_type=F32)
              + jnp.dot(h_hi, wlo_ref[...], preferred_element_type=F32))
    lane = lax.broadcasted_iota(jnp.int32, logits.shape, 1)
    valid = lane < n_experts
    lm = jnp.where(valid, logits, -jnp.inf)
    q = jnp.where(valid, jnp.exp(lm - jnp.max(lm, axis=1, keepdims=True)), 0.0)
    aff = q / jnp.sum(q, axis=1, keepdims=True)
    affn_ref[...] = aff
    afft_ref[...] = aff.T[0:n_experts, :]


def router_affinity(x, g, w_router, layer, *, batch, seq, tm=512):
    t, d = x.shape
    e = w_router.shape[2]
    w_pad = jnp.zeros((d, LANE), F32).at[:, :e].set(w_router[layer].astype(F32))
    w_hi = w_pad.astype(BF16)
    w_lo = (w_pad - w_hi.astype(F32)).astype(BF16)
    ns = seq // tm
    return pl.pallas_call(
        functools.partial(_router_kernel, n_experts=e),
        out_shape=(jax.ShapeDtypeStruct((batch, e, seq), F32), jax.ShapeDtypeStruct((t, LANE), F32)),
        grid=(batch, ns),
        in_specs=[
            pl.BlockSpec((tm, d), lambda b, i: (b * ns + i, 0)),
            pl.BlockSpec((1, d), lambda b, i: (0, 0)),
            pl.BlockSpec((d, LANE), lambda b, i: (0, 0)),
            pl.BlockSpec((d, LANE), lambda b, i: (0, 0)),
        ],
        out_specs=(pl.BlockSpec((None, e, tm), lambda b, i: (b, 0, i)),
                   pl.BlockSpec((tm, LANE), lambda b, i: (b * ns + i, 0))),
        compiler_params=_cparams(("parallel", "parallel")),
        name="router",
    )(x, g.reshape(1, d).astype(F32), w_hi, w_lo)


BISECT_ITERS = 152


def _route_kernel(afft_ref, affn_ref, idx_ref, gate_ref, slot_ref, *, cap):
    n_exp, seq = afft_ref.shape
    blk = LANE
    nblk = seq // blk
    capf = float(cap)

    xt = afft_ref[...]

    def bisect(_, carry):
        lo, hi = carry
        mid = 0.5 * (lo + hi)
        cnt = jnp.sum(jnp.where(xt >= mid, 1.0, 0.0), axis=1, keepdims=True)
        ge = cnt >= capf
        return jnp.where(ge, mid, lo), jnp.where(ge, hi, mid)

    lo, hi = lax.fori_loop(0, BISECT_ITERS, bisect,
                           (jnp.zeros((n_exp, 1), F32), jnp.full((n_exp, 1), 2.0, F32)))
    r_i = lax.broadcasted_iota(jnp.int32, (n_exp, LANE), 0)
    c_i = lax.broadcasted_iota(jnp.int32, (n_exp, LANE), 1)
    diag = r_i == c_i
    lane_ok = lax.broadcasted_iota(jnp.int32, (1, LANE), 1) < n_exp
    lo_r = jnp.where(lane_ok, jnp.sum(jnp.where(diag, lo, 0.0), axis=0, keepdims=True), 4.0)
    hi_r = jnp.where(lane_ok, jnp.sum(jnp.where(diag, hi, 0.0), axis=0, keepdims=True), 4.0)

    n_gt = jnp.sum(jnp.where(affn_ref[...] >= hi_r, 1.0, 0.0), axis=0, keepdims=True)
    need = capf - n_gt

    tr = lax.broadcasted_iota(jnp.int32, (blk, blk), 0)
    tc = lax.broadcasted_iota(jnp.int32, (blk, blk), 1)
    tril = jnp.where(tr >= tc, 1.0, 0.0).astype(BF16)
    carry_eq = jnp.zeros((1, LANE), F32)
    carry_sel = jnp.zeros((1, LANE), F32)
    for k in range(nblk):
        x = affn_ref[k * blk:(k + 1) * blk, :]
        gt = x >= hi_r
        eq = jnp.logical_and(x >= lo_r, jnp.logical_not(gt))
        pos_eq = jnp.dot(tril, jnp.where(eq, 1.0, 0.0).astype(BF16), preferred_element_type=F32) + carry_eq
        carry_eq = pos_eq[blk - 1:blk, :]
        sel = jnp.logical_or(gt, jnp.logical_and(eq, pos_eq <= need))
        cs = jnp.dot(tril, jnp.where(sel, 1.0, 0.0).astype(BF16), preferred_element_type=F32) + carry_sel
        carry_sel = cs[blk - 1:blk, :]
        slot_ref[k * blk:(k + 1) * blk, :] = jnp.where(sel, cs - 1.0, -1.0)

    s_iota = lax.broadcasted_iota(jnp.int32, (blk, cap), 1).astype(F32)
    t_iota = lax.broadcasted_iota(jnp.int32, (blk, 1), 0).astype(F32)
    for e in range(n_exp):
        def body(k, acc):
            acc_i, acc_g = acc
            r0 = pl.multiple_of(k * blk, blk)
            slot = slot_ref[pl.ds(r0, blk), :][:, e:e + 1]
            aff = affn_ref[pl.ds(r0, blk), :][:, e:e + 1]
            hit = jnp.broadcast_to(slot, (blk, cap)) == s_iota
            tok = t_iota + lax.convert_element_type(k * blk, F32)
            acc_i = acc_i + jnp.sum(jnp.where(hit, tok, 0.0), axis=0, keepdims=True)
            acc_g = acc_g + jnp.sum(jnp.where(hit, aff, 0.0), axis=0, keepdims=True)
            return acc_i, acc_g

        acc_i, acc_g = lax.fori_loop(0, nblk, body, (jnp.zeros((1, cap), F32), jnp.zeros((1, cap), F32)))
        idx_ref[e:e + 1, :] = acc_i.astype(jnp.int32)
        gate_ref[e:e + 1, :] = acc_g


def route(aff_t, aff_n, *, cap):
    batch, e, seq = aff_t.shape
    return pl.pallas_call(
        functools.partial(_route_kernel, cap=cap),
        out_shape=(jax.ShapeDtypeStruct((batch, e, cap), jnp.int32),
                   jax.ShapeDtypeStruct((batch, e, cap), F32)),
        grid=(batch,),
        in_specs=[pl.BlockSpec((None, e, seq), lambda b: (b, 0, 0)),
                  pl.BlockSpec((seq, LANE), lambda b: (b, 0))],
        out_specs=(pl.BlockSpec((None, e, cap), lambda b: (b, 0, 0)),
                   pl.BlockSpec((None, e, cap), lambda b: (b, 0, 0))),
        scratch_shapes=[pltpu.VMEM((seq, LANE), F32)],
        compiler_params=_cparams(("parallel",)),
        name="route",
    )(aff_t, aff_n)


def _row_copy(src_ref, src_row, dst_ref, dst_row, sem):
    return pltpu.make_async_copy(src_ref.at[pl.ds(src_row, 1)], dst_ref.at[pl.ds(dst_row, 1)], sem)


ROW_UNROLL = 8
NORM_ROWS = 128


def _expert_up_kernel(idx_ref, x_hbm, g_ref, wg_ref, wu_ref, o_ref, stage_ref, xn_ref, sem):
    e = pl.program_id(0)
    n_exp = pl.num_programs(0)
    rows = xn_ref.shape[0]

    def start_gather(expert):
        def body(j, c):
            for p in range(ROW_UNROLL):
                r = ROW_UNROLL * j + p
                _row_copy(x_hbm, idx_ref[expert * rows + r], stage_ref, r, sem).start(priority=p % 2)
            return c

        lax.fori_loop(0, rows // ROW_UNROLL, body, 0)

    @pl.when(pl.program_id(1) == 0)
    def _():
        @pl.when(e == 0)
        def _():
            start_gather(0)

        pltpu.make_async_copy(x_hbm.at[pl.ds(0, rows)], stage_ref, sem).wait()

        def norm(i, c):
            r = pl.multiple_of(i * NORM_ROWS, NORM_ROWS)
            xn_ref[pl.ds(r, NORM_ROWS), :] = _rms_rows(
                stage_ref[pl.ds(r, NORM_ROWS), :], g_ref[...]).astype(xn_ref.dtype)
            return c

        lax.fori_loop(0, rows // NORM_ROWS, norm, 0)

        @pl.when(e + 1 < n_exp)
        def _():
            start_gather(e + 1)

    xn = xn_ref[...]
    gate = jnp.dot(xn, wg_ref[...].astype(BF16), preferred_element_type=F32)
    up = jnp.dot(xn, wu_ref[...].astype(BF16), preferred_element_type=F32)
    o_ref[...] = (_silu(gate) * up).astype(o_ref.dtype)


def expert_up(idx_flat, x, g, w_gate_up, layer, *, rows, d_expert, tf=256):
    t, d = x.shape
    e = w_gate_up.shape[1]
    nf = d_expert // tf
    grid_spec = pltpu.PrefetchScalarGridSpec(
        num_scalar_prefetch=1,
        grid=(e, nf),
        in_specs=[
            pl.BlockSpec(memory_space=pl.ANY),
            pl.BlockSpec((1, d), lambda i, f, idx: (0, 0)),
            pl.BlockSpec((None, None, d, tf), lambda i, f, idx: (layer, i, 0, f)),
            pl.BlockSpec((None, None, d, tf), lambda i, f, idx: (layer, i, 0, nf + f)),
        ],
        out_specs=pl.BlockSpec((None, rows, tf), lambda i, f, idx: (i, 0, f)),
        scratch_shapes=[
            pltpu.VMEM((rows, d), F32),
            pltpu.VMEM((rows, d), BF16),
            pltpu.SemaphoreType.DMA(()),
        ],
    )
    return pl.pallas_call(
        _expert_up_kernel,
        out_shape=jax.ShapeDtypeStruct((e, rows, d_expert), BF16),
        grid_spec=grid_spec,
        compiler_params=_cparams(("arbitrary", "arbitrary")),
        name="expert_up",
    )(idx_flat, x, g.reshape(1, d).astype(F32), w_gate_up, w_gate_up)


def _expert_down_kernel(act_ref, w_ref, gate_ref, o_ref):
    y = jnp.dot(act_ref[...], w_ref[...].astype(BF16), preferred_element_type=F32) * gate_ref[...]
    o_ref[...] = y.reshape(o_ref.shape)


def expert_down(act, w_down, gates, layer, *, tn=1024):
    e, rows, f = act.shape
    d = w_down.shape[3]
    assert tn == SUBLANE * LANE
    return pl.pallas_call(
        _expert_down_kernel,
        out_shape=jax.ShapeDtypeStruct((e, d // tn, rows, SUBLANE, LANE), F32),
        grid=(e, d // tn),
        in_specs=[
            pl.BlockSpec((None, rows, f), lambda i, j: (i, 0, 0)),
            pl.BlockSpec((None, None, f, tn), lambda i, j: (layer, i, 0, j)),
            pl.BlockSpec((None, rows, 1), lambda i, j: (i, 0, 0)),
        ],
        out_specs=pl.BlockSpec((None, None, rows, SUBLANE, LANE), lambda i, j: (i, j, 0, 0, 0)),
        compiler_params=_cparams(("parallel", "arbitrary")),
        name="expert_down",
    )(act, w_down, gates)


COMBINE_CHUNK = 512
COMBINE_UNROLL = 16


def _combine_kernel(idx_ref, y3_ref, x_hbm, o_hbm, acc_ref, stage_ref, sem_in, sem_out,
                    *, seq, cap, n_batch):
    n = pl.program_id(0)
    b = pl.program_id(1)
    e = pl.program_id(2)
    n_exp = pl.num_programs(2)
    cn = stage_ref.shape[2]
    ch = stage_ref.shape[1]
    n_chunks = seq // ch
    col = pl.multiple_of(n * cn, cn)

    def x_copy(c):
        row = pl.multiple_of(b * seq + c * ch, ch)
        return pltpu.make_async_copy(x_hbm.at[pl.ds(row, ch), pl.ds(col, cn)], stage_ref.at[c % 2],
                                     sem_in.at[c % 2])

    def o_copy(c):
        row = pl.multiple_of(b * seq + c * ch, ch)
        return pltpu.make_async_copy(stage_ref.at[c % 2], o_hbm.at[pl.ds(row, ch), pl.ds(col, cn)],
                                     sem_out.at[c % 2])

    @pl.when(e == 0)
    def _():
        x_copy(0).start()
        for c in range(n_chunks):
            x_copy(c).wait()
            if c + 1 < n_chunks:
                x_copy(c + 1).start()
            acc_ref[c * ch:(c + 1) * ch] = stage_ref[c % 2].reshape(ch, SUBLANE, LANE)

    base = (e * n_batch + b) * cap

    def body(i, c):
        toks = [idx_ref[base + COMBINE_UNROLL * i + u] for u in range(COMBINE_UNROLL)]
        vals = [acc_ref[toks[u]] + y3_ref[COMBINE_UNROLL * i + u] for u in range(COMBINE_UNROLL)]
        for u in range(COMBINE_UNROLL):
            acc_ref[toks[u]] = vals[u]
        return c

    lax.fori_loop(0, cap // COMBINE_UNROLL, body, 0)

    @pl.when(e == n_exp - 1)
    def _():
        for c in range(n_chunks):
            if c >= 2:
                o_copy(c - 2).wait()
            stage_ref[c % 2] = acc_ref[c * ch:(c + 1) * ch].reshape(ch, cn)
            o_copy(c).start()
        for c in range(max(n_chunks - 2, 0), n_chunks):
            o_copy(c).wait()


def moe_combine(idx_local, y, x, *, batch, seq, cap):
    e = y.shape[0]
    t, d = x.shape
    cn = SUBLANE * LANE
    grid_spec = pltpu.PrefetchScalarGridSpec(
        num_scalar_prefetch=1,
        grid=(d // cn, batch, e),
        in_specs=[
            pl.BlockSpec((None, None, cap, SUBLANE, LANE), lambda n, b, i, idx: (i, n, b, 0, 0)),
            pl.BlockSpec(memory_space=pl.ANY),
        ],
        out_specs=pl.BlockSpec(memory_space=pl.ANY),
        scratch_shapes=[
            pltpu.VMEM((seq, SUBLANE, LANE), F32),
            pltpu.VMEM((2, min(COMBINE_CHUNK, seq), cn), F32),
            pltpu.SemaphoreType.DMA((2,)),
            pltpu.SemaphoreType.DMA((2,)),
        ],
    )
    return pl.pallas_call(
        functools.partial(_combine_kernel, seq=seq, cap=cap, n_batch=batch),
        out_shape=jax.ShapeDtypeStruct((t, d), F32),
        grid_spec=grid_spec,
        compiler_params=_cparams(("arbitrary", "arbitrary", "arbitrary")),
        name="moe_combine",
    )(idx_local, y, x)


def _final_norm_kernel(x_ref, g_ref, o_ref):
    o_ref[...] = _rms_rows(x_ref[...], g_ref[...])


def final_norm(x, g, *, tm=512):
    t, d = x.shape
    return pl.pallas_call(
        _final_norm_kernel,
        out_shape=jax.ShapeDtypeStruct((t, d), F32),
        grid=(t // tm,),
        in_specs=[pl.BlockSpec((tm, d), lambda i: (i, 0)), pl.BlockSpec((1, d), lambda i: (0, 0))],
        out_specs=pl.BlockSpec((tm, d), lambda i: (i, 0)),
        compiler_params=_cparams(("parallel",)),
        name="final_norm",
    )(x, g.reshape(1, d).astype(F32))


def kernel(x, mem, norm_mix_g, w_in, conv_w, conv_b, dt_bias, a_log, d_skip, ssd_norm_g, gmlp_norm_g,
           gmlp_ws, gmlp_bs, w_out, norm_xattn_g, norm_mem_g, w_q, w_kv, w_o, norm_moe_g, w_router,
           w_gate_up, w_down, final_norm_g):
    batch, seq, d_model = x.shape
    mem_len = mem.shape[1]
    depth = w_in.shape[0]
    n_heads = dt_bias.shape[2]
    d_ssd = n_heads * SSD_HEAD_DIM
    conv_ch = conv_w.shape[2]
    d_gmlp = gmlp_norm_g.shape[1]
    d_expert = w_down.shape[2]
    n_experts = w_router.shape[2]
    cap = CAPACITY_FACTOR * seq // n_experts
    t = batch * seq
    o_dt = d_ssd + conv_ch
    o_u = o_dt + 2 * n_heads

    xf = x.reshape(t, d_model)
    memf = mem.reshape(batch * mem_len, d_model)
    w_in_t = jnp.swapaxes(w_in, 1, 2).reshape(depth * w_in.shape[2], d_model)
    for l in range(depth):
        h = norm_cast(xf, norm_mix_g[l])
        r0 = l * w_in.shape[2]
        proj = matmul_nt(h, w_in_t, r0, o_dt)
        proj_uv = matmul_nt(h, w_in_t, r0 + o_u, 2 * d_gmlp)
        dt_raw = matmul_nt(h, w_in_t, r0 + o_dt, LANE, out_dtype=F32)
        xbc = conv_silu(proj, conv_w[l], conv_b[l], batch=batch, seq=seq, col0=d_ssd)
        y_f, y_b = ssd_scan(xbc, dt_raw, dt_bias[l], a_log[l], batch=batch, seq=seq, d_ssd=d_ssd)
        y_cat = mixer_out(proj, proj_uv, xbc, y_f, y_b, d_skip[l], ssd_norm_g[l], gmlp_norm_g[l],
                          gmlp_ws[l], gmlp_bs[l], d_ssd=d_ssd, d_gmlp=d_gmlp)
        xf, xg, ss = matmul(y_cat, w_out, l, res=xf, next_gain=norm_xattn_g[l])

        q = matmul(xg, w_q, l, row_ss=ss)
        kv = matmul(norm_cast(memf, norm_mem_g[l]), w_kv, l)
        o = cross_attention(q, kv, batch=batch, seq=seq, mem_len=mem_len)
        xf = matmul(o, w_o, l, res=xf)

        aff_t, aff_n = router_affinity(xf, norm_moe_g[l], w_router, l, batch=batch, seq=seq)
        idx, gate = route(aff_t, aff_n, cap=cap)
        idx_local = jnp.swapaxes(idx, 0, 1)
        idx_flat = (idx_local + (jnp.arange(batch, dtype=jnp.int32) * seq)[None, :, None]).reshape(-1)
        gates = jnp.swapaxes(gate, 0, 1).reshape(n_experts, batch * cap, 1)
        act = expert_up(idx_flat, xf, norm_moe_g[l], w_gate_up, l, rows=batch * cap, d_expert=d_expert)
        y = expert_down(act, w_down, gates, l)
        xf = moe_combine(idx_local.reshape(-1), y, xf, batch=batch, seq=seq, cap=cap)
    out = final_norm(xf, final_norm_g)
    return out.reshape(batch, seq, d_model)
```

```python
import functools

import jax
import jax.numpy as jnp
from jax import lax
from jax.experimental import pallas as pl
from jax.experimental.pallas import tpu as pltpu

F32 = jnp.float32
BF16 = jnp.bfloat16

RMS_EPS = 1e-6
SSD_HEAD_DIM = 64
SSD_GROUPS = 8
D_STATE = 128
D_CONV = 5
CHUNK = 128
GMLP_GROUP_WIDTH = 128
XATTN_HEADS = 4
N_EXPERTS = 16
CAPACITY_FACTOR = 2
LANE = 128
SUBLANE = 8
VMEM_LIMIT = 56 * 1024 * 1024


def _cparams(sem, vmem=VMEM_LIMIT):
    return pltpu.CompilerParams(dimension_semantics=sem, vmem_limit_bytes=vmem)


def _silu(x):
    return x * (1.0 / (1.0 + jnp.exp(-x)))


def _rms_rows(x, g):
    ms = jnp.mean(x * x, axis=-1, keepdims=True)
    return x * lax.rsqrt(ms + RMS_EPS) * g


def _norm_cast_kernel(x_ref, g_ref, o_ref):
    o_ref[...] = _rms_rows(x_ref[...], g_ref[...]).astype(o_ref.dtype)


def norm_cast(x, g, *, tm=512):
    m, k = x.shape
    tm = min(tm, m)
    return pl.pallas_call(
        _norm_cast_kernel,
        out_shape=jax.ShapeDtypeStruct((m, k), BF16),
        grid=(m // tm,),
        in_specs=[pl.BlockSpec((tm, k), lambda i: (i, 0)), pl.BlockSpec((1, k), lambda i: (0, 0))],
        out_specs=pl.BlockSpec((tm, k), lambda i: (i, 0)),
        compiler_params=_cparams(("parallel",)),
        name="norm_cast",
    )(x, g.reshape(1, k).astype(F32))


def _mm_kernel(a_ref, w_ref, o_ref):
    o_ref[...] = jnp.dot(a_ref[...], w_ref[...].astype(BF16),
                         preferred_element_type=F32).astype(o_ref.dtype)


def _mm_nt_kernel(a_ref, wt_ref, o_ref):
    o_ref[...] = lax.dot_general(a_ref[...], wt_ref[...].astype(BF16), (((1,), (1,)), ((), ())),
                                 preferred_element_type=F32).astype(o_ref.dtype)


def matmul_nt(a, wt, row0, n, *, out_dtype=BF16, tm=1024, tn=512):
    m, k = a.shape
    tm = min(tm, m)
    tn = min(tn, n)
    assert m % tm == 0 and n % tn == 0 and row0 % 8 == 0 and wt.shape[1] == k
    return pl.pallas_call(
        _mm_nt_kernel,
        out_shape=jax.ShapeDtypeStruct((m, n), out_dtype),
        grid=(m // tm, n // tn),
        in_specs=[pl.BlockSpec((tm, k), lambda i, j: (i, 0)),
                  pl.BlockSpec((pl.Element(tn), pl.Element(k)),
                               lambda i, j: (pl.multiple_of(row0 + j * tn, 8), 0))],
        out_specs=pl.BlockSpec((tm, tn), lambda i, j: (i, j)),
        compiler_params=_cparams(("parallel", "arbitrary")),
        name="matmul_nt",
    )(a, wt)


def _mm_res_kernel(a_ref, w_ref, r_ref, o_ref):
    o_ref[...] = r_ref[...] + jnp.dot(a_ref[...], w_ref[...].astype(BF16), preferred_element_type=F32)


def _mm_res_norm_kernel(a_ref, w_ref, r_ref, g_ref, o_ref, xg_ref, ss_ref):
    x = r_ref[...] + jnp.dot(a_ref[...], w_ref[...].astype(BF16), preferred_element_type=F32)
    o_ref[...] = x
    xg_ref[...] = (x * g_ref[...]).astype(xg_ref.dtype)

    @pl.when(pl.program_id(1) == 0)
    def _():
        ss_ref[...] = jnp.zeros_like(ss_ref)

    ss_ref[...] += jnp.sum(x * x, axis=1, keepdims=True)


def _mm_rowscale_kernel(a_ref, w_ref, ss_ref, o_ref):
    inv = lax.rsqrt(ss_ref[...] * (1.0 / a_ref.shape[1]) + RMS_EPS)
    o_ref[...] = (jnp.dot(a_ref[...], w_ref[...].astype(BF16), preferred_element_type=F32)
                  * inv).astype(o_ref.dtype)


def matmul(a, w, layer, *, n_cols=None, res=None, out_dtype=BF16, tm=1024, tn=512, next_gain=None,
           row_ss=None):
    m, k = a.shape
    n = w.shape[2] if n_cols is None else n_cols
    tm = min(tm, m)
    tn = min(tn, n)
    assert m % tm == 0 and n % tn == 0
    a_spec = pl.BlockSpec((tm, k), lambda i, j: (i, 0))
    w_spec = pl.BlockSpec((None, k, tn), lambda i, j: (layer, 0, j))
    o_spec = pl.BlockSpec((tm, tn), lambda i, j: (i, j))
    if next_gain is not None:
        return pl.pallas_call(
            _mm_res_norm_kernel,
            out_shape=(jax.ShapeDtypeStruct((m, n), F32), jax.ShapeDtypeStruct((m, n), BF16),
                       jax.ShapeDtypeStruct((m, 1), F32)),
            grid=(m // tm, n // tn),
            in_specs=[a_spec, w_spec, o_spec, pl.BlockSpec((1, tn), lambda i, j: (0, j))],
            out_specs=(o_spec, o_spec, pl.BlockSpec((tm, 1), lambda i, j: (i, 0))),
            compiler_params=_cparams(("parallel", "arbitrary")),
            name="matmul_residual_norm",
        )(a, w, res, next_gain.reshape(1, n).astype(F32))
    if row_ss is not None:
        return pl.pallas_call(
            _mm_rowscale_kernel,
            out_shape=jax.ShapeDtypeStruct((m, n), out_dtype),
            grid=(m // tm, n // tn),
            in_specs=[a_spec, w_spec, pl.BlockSpec((tm, 1), lambda i, j: (i, 0))],
            out_specs=o_spec,
            compiler_params=_cparams(("parallel", "arbitrary")),
            name="matmul_rowscale",
        )(a, w, row_ss)
    if res is None:
        return pl.pallas_call(
            _mm_kernel,
            out_shape=jax.ShapeDtypeStruct((m, n), out_dtype),
            grid=(m // tm, n // tn),
            in_specs=[a_spec, w_spec],
            out_specs=o_spec,
            compiler_params=_cparams(("parallel", "arbitrary")),
            name="matmul",
        )(a, w)
    return pl.pallas_call(
        _mm_res_kernel,
        out_shape=jax.ShapeDtypeStruct((m, n), F32),
        grid=(m // tm, n // tn),
        in_specs=[a_spec, w_spec, o_spec],
        out_specs=o_spec,
        compiler_params=_cparams(("parallel", "arbitrary")),
        name="matmul_residual",
    )(a, w, res)


HALO = 16


def _conv_kernel(prev_ref, cur_ref, next_ref, w_ref, b_ref, o_ref, ext_ref, *, n_seq_tiles):
    ts = cur_ref.shape[0]
    s = pl.program_id(1)
    prev = prev_ref[...].astype(F32)
    nxt = next_ref[...].astype(F32)
    ext_ref[0:HALO, :] = jnp.where(s == 0, 0.0, prev)
    ext_ref[HALO:HALO + ts, :] = cur_ref[...].astype(F32)
    ext_ref[HALO + ts:, :] = jnp.where(s == n_seq_tiles - 1, 0.0, nxt)
    acc = jnp.zeros(cur_ref.shape, F32) + b_ref[...]
    for k in range(D_CONV):
        off = HALO - D_CONV // 2 + k
        acc = acc + ext_ref[off:off + ts, :] * w_ref[k:k + 1, :]
    o_ref[...] = _silu(acc).astype(o_ref.dtype)


def conv_silu(proj, conv_w, conv_b, *, batch, seq, col0, ts=512, tc=1024):
    c = conv_w.shape[1]
    t = batch * seq
    nst = seq // ts
    hb = ts // HALO
    cb0 = col0 // tc
    assert col0 % tc == 0 and c % tc == 0 and seq % ts == 0
    last_halo = t // HALO - 1

    def prev_map(b, s, j):
        return (jnp.maximum((b * nst + s) * hb - 1, 0), cb0 + j)

    def next_map(b, s, j):
        return (jnp.minimum((b * nst + s + 1) * hb, last_halo), cb0 + j)

    return pl.pallas_call(
        functools.partial(_conv_kernel, n_seq_tiles=nst),
        out_shape=jax.ShapeDtypeStruct((t, c), BF16),
        grid=(batch, nst, c // tc),
        in_specs=[
            pl.BlockSpec((HALO, tc), prev_map),
            pl.BlockSpec((ts, tc), lambda b, s, j: (b * nst + s, cb0 + j)),
            pl.BlockSpec((HALO, tc), next_map),
            pl.BlockSpec((D_CONV, tc), lambda b, s, j: (0, j)),
            pl.BlockSpec((1, tc), lambda b, s, j: (0, j)),
        ],
        out_specs=pl.BlockSpec((ts, tc), lambda b, s, j: (b * nst + s, j)),
        scratch_shapes=[pltpu.VMEM((ts + 2 * HALO, tc), F32)],
        compiler_params=_cparams(("parallel", "parallel", "parallel")),
        name="conv_silu",
    )(proj, proj, proj, conv_w.astype(F32), conv_b.reshape(1, c).astype(F32))


def _ssd_direction(xs_ref, b_ref, c_ref, dt_ref, bias_ref, alog_ref, expand_ref, st_ref, y_ref,
                   *, reverse, col0, n_heads):
    L = CHUNK
    hp = SSD_HEAD_DIM
    heads_per_group = n_heads // SSD_GROUPS
    gw = heads_per_group * hp
    row = lax.broadcasted_iota(jnp.int32, (L, L), 0)
    col = lax.broadcasted_iota(jnp.int32, (L, L), 1)
    keep = (col >= row) if reverse else (col <= row)
    tri = jnp.where(keep, 1.0, 0.0).astype(F32)

    xraw = dt_ref[...] + bias_ref[...]
    dt = jnp.maximum(xraw, 0.0) + jnp.log1p(jnp.exp(-jnp.abs(xraw)))
    a = -jnp.exp(alog_ref[...])
    dta = dt * a
    acs = jnp.dot(tri, dta, preferred_element_type=F32, precision=lax.Precision.HIGHEST)
    acs_t = acs.T
    total = acs[0:1, :] if reverse else acs[L - 1:L, :]
    dte = jnp.exp(total - acs)
    ea = jnp.exp(acs)
    cdec = jnp.broadcast_to(jnp.exp(total), (8, LANE))

    stack = jnp.concatenate([dt, dte, ea, cdec], axis=0).astype(BF16)
    ex = jnp.dot(stack, expand_ref[...], preferred_element_type=F32)
    dt_x = ex[0:L]
    dte_x = ex[L:2 * L]
    ea_x = ex[2 * L:3 * L]
    cdec_x = ex[3 * L:3 * L + 1]

    lane_head = lax.broadcasted_iota(jnp.int32, (L, gw), 1) // hp
    for g in range(SSD_GROUPS):
        sl = slice(g * gw, (g + 1) * gw)
        xs_g = xs_ref[:, sl].astype(F32)
        xdt = xs_g * dt_x[:, sl]
        xdt_bf = xdt.astype(BF16)
        xdtw_bf = (xdt * dte_x[:, sl]).astype(BF16)
        b_g = b_ref[:, g * D_STATE:(g + 1) * D_STATE]
        c_g = c_ref[:, g * D_STATE:(g + 1) * D_STATE]
        cb = lax.dot_general(c_g, b_g, (((1,), (1,)), ((), ())), preferred_element_type=F32)
        y_g = jnp.zeros((L, gw), F32)
        for r in range(heads_per_group):
            hc = col0 + g * heads_per_group + r
            seg = acs[:, hc:hc + 1] - acs_t[hc:hc + 1, :]
            decay = jnp.exp(jnp.where(keep, seg, -jnp.inf))
            m = (cb * decay).astype(BF16)
            x_r = jnp.where(lane_head == r, xdt_bf, jnp.zeros_like(xdt_bf))
            y_g = y_g + jnp.dot(m, x_r, preferred_element_type=F32)
        st = st_ref[g]
        y_off = jnp.dot(c_g, st.astype(BF16), preferred_element_type=F32) * ea_x[:, sl]
        y_ref[:, sl] = (y_g + y_off).astype(y_ref.dtype)
        b_t = b_g.astype(F32).T.astype(BF16)
        st_ref[g] = st * cdec_x[:, sl] + jnp.dot(b_t, xdtw_bf, preferred_element_type=F32)


def _ssd_kernel(xs_f, b_f, c_f, dt_f, xs_b, b_b, c_b, dt_b, bias_ref, alog_ref, expand_ref,
                yf_ref, yb_ref, stf_ref, stb_ref, *, n_heads):
    @pl.when(pl.program_id(1) == 0)
    def _():
        stf_ref[...] = jnp.zeros_like(stf_ref)
        stb_ref[...] = jnp.zeros_like(stb_ref)

    _ssd_direction(xs_f, b_f, c_f, dt_f, bias_ref, alog_ref, expand_ref.at[0], stf_ref, yf_ref,
                   reverse=False, col0=0, n_heads=n_heads)
    _ssd_direction(xs_b, b_b, c_b, dt_b, bias_ref, alog_ref, expand_ref.at[1], stb_ref, yb_ref,
                   reverse=True, col0=n_heads, n_heads=n_heads)


def ssd_scan(xbc, dt_raw, dt_bias, a_log, *, batch, seq, d_ssd):
    t = batch * seq
    nc = seq // CHUNK
    n_heads = d_ssd // SSD_HEAD_DIM
    gn = SSD_GROUPS * D_STATE
    assert d_ssd % gn == 0 and 2 * n_heads <= LANE
    xb = d_ssd // gn
    bias = jnp.zeros((1, LANE), F32).at[0, :2 * n_heads].set(dt_bias.reshape(-1).astype(F32))
    alog = jnp.zeros((1, LANE), F32).at[0, :2 * n_heads].set(a_log.reshape(-1).astype(F32))
    lane_h = jnp.arange(d_ssd) // SSD_HEAD_DIM
    expand = jnp.stack([
        (jnp.arange(LANE)[:, None] == (d * n_heads + lane_h)[None, :]) for d in range(2)
    ]).astype(BF16)

    def fwd(b, c):
        return b * nc + c

    def bwd(b, c):
        return b * nc + (nc - 1 - c)

    def specs(rmap):
        return [
            pl.BlockSpec((CHUNK, d_ssd), lambda b, c: (rmap(b, c), 0)),
            pl.BlockSpec((CHUNK, gn), lambda b, c: (rmap(b, c), xb)),
            pl.BlockSpec((CHUNK, gn), lambda b, c: (rmap(b, c), xb + 1)),
            pl.BlockSpec((CHUNK, LANE), lambda b, c: (rmap(b, c), 0)),
        ]

    const2 = lambda b, c: (0, 0)
    gw = d_ssd // SSD_GROUPS
    return pl.pallas_call(
        functools.partial(_ssd_kernel, n_heads=n_heads),
        out_shape=(jax.ShapeDtypeStruct((t, d_ssd), BF16), jax.ShapeDtypeStruct((t, d_ssd), BF16)),
        grid=(batch, nc),
        in_specs=specs(fwd) + specs(bwd) + [
            pl.BlockSpec((1, LANE), const2),
            pl.BlockSpec((1, LANE), const2),
            pl.BlockSpec((2, LANE, d_ssd), lambda b, c: (0, 0, 0)),
        ],
        out_specs=(
            pl.BlockSpec((CHUNK, d_ssd), lambda b, c: (fwd(b, c), 0)),
            pl.BlockSpec((CHUNK, d_ssd), lambda b, c: (bwd(b, c), 0)),
        ),
        scratch_shapes=[pltpu.VMEM((SSD_GROUPS, D_STATE, gw), F32),
                        pltpu.VMEM((SSD_GROUPS, D_STATE, gw), F32)],
        compiler_params=_cparams(("parallel", "arbitrary")),
        name="ssd_scan",
    )(xbc, xbc, xbc, dt_raw, xbc, xbc, xbc, dt_raw, bias, alog, expand)


def _gelu(x):
    return 0.5 * x * (1.0 + lax.erf(x * (2.0 ** -0.5)))


def _mixer_out_kernel(z_ref, xs_ref, yf_ref, yb_ref, u_ref, v_ref, dskip_ref, sg_ref, gg_ref,
                      ws_ref, bs_ref, o_ref, *, d_ssd):
    gw = d_ssd // SSD_GROUPS
    y = (yf_ref[...].astype(F32) + yb_ref[...].astype(F32)
         + xs_ref[...].astype(F32) * dskip_ref[...]) * _silu(z_ref[...].astype(F32))
    for g in range(SSD_GROUPS):
        sl = slice(g * gw, (g + 1) * gw)
        o_ref[:, sl] = _rms_rows(y[:, sl], sg_ref[:, sl]).astype(o_ref.dtype)

    uu = _gelu(u_ref[...].astype(F32))
    vn = _rms_rows(_gelu(v_ref[...].astype(F32)), gg_ref[...]).astype(BF16)
    bs = bs_ref[...]
    for g in range(ws_ref.shape[0]):
        sl = slice(g * GMLP_GROUP_WIDTH, (g + 1) * GMLP_GROUP_WIDTH)
        sp = jnp.dot(ws_ref[g], vn[:, sl], preferred_element_type=F32) + bs[:, g:g + 1]
        o_ref[:, d_ssd + g * GMLP_GROUP_WIDTH:d_ssd + (g + 1) * GMLP_GROUP_WIDTH] = (
            uu[:, sl] * sp).astype(o_ref.dtype)


def mixer_out(proj, proj_uv, xbc, y_f, y_b, d_skip, ssd_norm_g, gmlp_norm_g, gmlp_ws, gmlp_bs, *, d_ssd,
              d_gmlp):
    t = proj.shape[0]
    assert d_ssd == d_gmlp
    w = d_ssd
    n_groups = gmlp_ws.shape[0]
    dskip = jnp.repeat(d_skip.astype(F32), SSD_HEAD_DIM).reshape(1, d_ssd)
    row = lambda i: (i, 0)
    const = lambda i: (0, 0)
    return pl.pallas_call(
        functools.partial(_mixer_out_kernel, d_ssd=d_ssd),
        out_shape=jax.ShapeDtypeStruct((t, d_ssd + d_gmlp), BF16),
        grid=(t // CHUNK,),
        in_specs=[
            pl.BlockSpec((CHUNK, w), row),
            pl.BlockSpec((CHUNK, w), row),
            pl.BlockSpec((CHUNK, w), row),
            pl.BlockSpec((CHUNK, w), row),
            pl.BlockSpec((CHUNK, w), row),
            pl.BlockSpec((CHUNK, w), lambda i: (i, 1)),
            pl.BlockSpec((1, w), const),
            pl.BlockSpec((1, w), const),
            pl.BlockSpec((1, w), const),
            pl.BlockSpec((n_groups, CHUNK, CHUNK), lambda i: (0, 0, 0)),
            pl.BlockSpec((CHUNK, n_groups), const),
        ],
        out_specs=pl.BlockSpec((CHUNK, d_ssd + d_gmlp), row),
        compiler_params=_cparams(("parallel",)),
        name="mixer_out",
    )(proj, xbc, y_f, y_b, proj_uv, proj_uv, dskip, ssd_norm_g.reshape(1, -1).astype(F32),
      gmlp_norm_g.reshape(1, -1).astype(F32), gmlp_ws.astype(BF16), gmlp_bs.T.astype(F32))


def _xattn_kernel(q_ref, k_ref, v_ref, o_ref, *, scale):
    s = lax.dot_general(q_ref[...], k_ref[...], (((1,), (1,)), ((), ())),
                        preferred_element_type=F32) * scale
    p = jnp.exp(s - jnp.max(s, axis=-1, keepdims=True))
    denom = jnp.sum(p, axis=-1, keepdims=True)
    o = jnp.dot(p.astype(BF16), v_ref[...], preferred_element_type=F32)
    o_ref[...] = (o / denom).astype(o_ref.dtype)


def cross_attention(q, kv, *, batch, seq, mem_len, tq=1024):
    t, d = q.shape
    hd = d // XATTN_HEADS
    nq = seq // tq
    return pl.pallas_call(
        functools.partial(_xattn_kernel, scale=hd ** -0.5),
        out_shape=jax.ShapeDtypeStruct((t, d), BF16),
        grid=(batch, XATTN_HEADS, nq),
        in_specs=[
            pl.BlockSpec((tq, hd), lambda b, h, i: (b * nq + i, h)),
            pl.BlockSpec((mem_len, hd), lambda b, h, i: (b, h)),
            pl.BlockSpec((mem_len, hd), lambda b, h, i: (b, XATTN_HEADS + h)),
        ],
        out_specs=pl.BlockSpec((tq, hd), lambda b, h, i: (b * nq + i, h)),
        compiler_params=_cparams(("parallel", "parallel", "parallel")),
        name="cross_attention",
    )(q, kv, kv)


def _router_kernel(x_ref, g_ref, whi_ref, wlo_ref, afft_ref, affn_ref, *, n_experts):
    h = _rms_rows(x_ref[...], g_ref[...])
    h_hi = h.astype(BF16)
    h_lo = (h - h_hi.astype(F32)).astype(BF16)
    logits = (jnp.dot(h_hi, whi_ref[...], preferred_element_type=F32)
              + jnp.dot(h_lo, whi_ref[...], preferred_element_type=F32)
              + jnp.dot(h_hi, wlo_ref[...], preferred_element_type=F32))
    lane = lax.broadcasted_iota(jnp.int32, logits.shape, 1)
    valid = lane < n_experts
    lm = jnp.where(valid, logits, -jnp.inf)
    q = jnp.where(valid, jnp.exp(lm - jnp.max(lm, axis=1, keepdims=True)), 0.0)
    aff = q / jnp.sum(q, axis=1, keepdims=True)
    affn_ref[...] = aff
    afft_ref[...] = aff.T[0:n_experts, :]


def router_affinity(x, g, w_router, layer, *, batch, seq, tm=512):
    t, d = x.shape
    e = w_router.shape[2]
    w_pad = jnp.zeros((d, LANE), F32).at[:, :e].set(w_router[layer].astype(F32))
    w_hi = w_pad.astype(BF16)
    w_lo = (w_pad - w_hi.astype(F32)).astype(BF16)
    ns = seq // tm
    return pl.pallas_call(
        functools.partial(_router_kernel, n_experts=e),
        out_shape=(jax.ShapeDtypeStruct((batch, e, seq), F32), jax.ShapeDtypeStruct((t, LANE), F32)),
        grid=(batch, ns),
        in_specs=[
            pl.BlockSpec((tm, d), lambda b, i: (b * ns + i, 0)),
            pl.BlockSpec((1, d), lambda b, i: (0, 0)),
            pl.BlockSpec((d, LANE), lambda b, i: (0, 0)),
            pl.BlockSpec((d, LANE), lambda b, i: (0, 0)),
        ],
        out_specs=(pl.BlockSpec((None, e, tm), lambda b, i: (b, 0, i)),
                   pl.BlockSpec((tm, LANE), lambda b, i: (b * ns + i, 0))),
        compiler_params=_cparams(("parallel", "parallel")),
        name="router",
    )(x, g.reshape(1, d).astype(F32), w_hi, w_lo)


BISECT_ITERS = 152


def _route_kernel(afft_ref, affn_ref, idx_ref, gate_ref, slot_ref, *, cap):
    n_exp, seq = afft_ref.shape
    blk = LANE
    nblk = seq // blk
    capf = float(cap)

    xt = afft_ref[...]

    def bisect(_, carry):
        lo, hi = carry
        mid = 0.5 * (lo + hi)
        cnt = jnp.sum(jnp.where(xt >= mid, 1.0, 0.0), axis=1, keepdims=True)
        ge = cnt >= capf
        return jnp.where(ge, mid, lo), jnp.where(ge, hi, mid)

    lo, hi = lax.fori_loop(0, BISECT_ITERS, bisect,
                           (jnp.zeros((n_exp, 1), F32), jnp.full((n_exp, 1), 2.0, F32)))
    r_i = lax.broadcasted_iota(jnp.int32, (n_exp, LANE), 0)
    c_i = lax.broadcasted_iota(jnp.int32, (n_exp, LANE), 1)
    diag = r_i == c_i
    lane_ok = lax.broadcasted_iota(jnp.int32, (1, LANE), 1) < n_exp
    lo_r = jnp.where(lane_ok, jnp.sum(jnp.where(diag, lo, 0.0), axis=0, keepdims=True), 4.0)
    hi_r = jnp.where(lane_ok, jnp.sum(jnp.where(diag, hi, 0.0), axis=0, keepdims=True), 4.0)

    n_gt = jnp.sum(jnp.where(affn_ref[...] >= hi_r, 1.0, 0.0), axis=0, keepdims=True)
    need = capf - n_gt

    tr = lax.broadcasted_iota(jnp.int32, (blk, blk), 0)
    tc = lax.broadcasted_iota(jnp.int32, (blk, blk), 1)
    tril = jnp.where(tr >= tc, 1.0, 0.0).astype(BF16)
    carry_eq = jnp.zeros((1, LANE), F32)
    carry_sel = jnp.zeros((1, LANE), F32)
    for k in range(nblk):
        x = affn_ref[k * blk:(k + 1) * blk, :]
        gt = x >= hi_r
        eq = jnp.logical_and(x >= lo_r, jnp.logical_not(gt))
        pos_eq = jnp.dot(tril, jnp.where(eq, 1.0, 0.0).astype(BF16), preferred_element_type=F32) + carry_eq
        carry_eq = pos_eq[blk - 1:blk, :]
        sel = jnp.logical_or(gt, jnp.logical_and(eq, pos_eq <= need))
        cs = jnp.dot(tril, jnp.where(sel, 1.0, 0.0).astype(BF16), preferred_element_type=F32) + carry_sel
        carry_sel = cs[blk - 1:blk, :]
        slot_ref[k * blk:(k + 1) * blk, :] = jnp.where(sel, cs - 1.0, -1.0)

    s_iota = lax.broadcasted_iota(jnp.int32, (blk, cap), 1).astype(F32)
    t_iota = lax.broadcasted_iota(jnp.int32, (blk, 1), 0).astype(F32)
    for e in range(n_exp):
        def body(k, acc):
            acc_i, acc_g = acc
            r0 = pl.multiple_of(k * blk, blk)
            slot = slot_ref[pl.ds(r0, blk), :][:, e:e + 1]
            aff = affn_ref[pl.ds(r0, blk), :][:, e:e + 1]
            hit = jnp.broadcast_to(slot, (blk, cap)) == s_iota
            tok = t_iota + lax.convert_element_type(k * blk, F32)
            acc_i = acc_i + jnp.sum(jnp.where(hit, tok, 0.0), axis=0, keepdims=True)
            acc_g = acc_g + jnp.sum(jnp.where(hit, aff, 0.0), axis=0, keepdims=True)
            return acc_i, acc_g

        acc_i, acc_g = lax.fori_loop(0, nblk, body, (jnp.zeros((1, cap), F32), jnp.zeros((1, cap), F32)))
        idx_ref[e:e + 1, :] = acc_i.astype(jnp.int32)
        gate_ref[e:e + 1, :] = acc_g


def route(aff_t, aff_n, *, cap):
    batch, e, seq = aff_t.shape
    return pl.pallas_call(
        functools.partial(_route_kernel, cap=cap),
        out_shape=(jax.ShapeDtypeStruct((batch, e, cap), jnp.int32),
                   jax.ShapeDtypeStruct((batch, e, cap), F32)),
        grid=(batch,),
        in_specs=[pl.BlockSpec((None, e, seq), lambda b: (b, 0, 0)),
                  pl.BlockSpec((seq, LANE), lambda b: (b, 0))],
        out_specs=(pl.BlockSpec((None, e, cap), lambda b: (b, 0, 0)),
                   pl.BlockSpec((None, e, cap), lambda b: (b, 0, 0))),
        scratch_shapes=[pltpu.VMEM((seq, LANE), F32)],
        compiler_params=_cparams(("parallel",)),
        name="route",
    )(aff_t, aff_n)


def _row_copy(src_ref, src_row, dst_ref, dst_row, sem):
    return pltpu.make_async_copy(src_ref.at[pl.ds(src_row, 1)], dst_ref.at[pl.ds(dst_row, 1)], sem)


ROW_UNROLL = 8
NORM_ROWS = 128


def _expert_up_kernel(idx_ref, x_hbm, g_ref, wg_ref, wu_ref, o_ref, stage_ref, xn_ref, sem):
    e = pl.program_id(0)
    n_exp = pl.num_programs(0)
    rows = xn_ref.shape[0]

    def start_gather(expert):
        def body(j, c):
            for p in range(ROW_UNROLL):
                r = ROW_UNROLL * j + p
                _row_copy(x_hbm, idx_ref[expert * rows + r], stage_ref, r, sem).start(priority=p % 2)
            return c

        lax.fori_loop(0, rows // ROW_UNROLL, body, 0)

    @pl.when(pl.program_id(1) == 0)
    def _():
        @pl.when(e == 0)
        def _():
            start_gather(0)

        pltpu.make_async_copy(x_hbm.at[pl.ds(0, rows)], stage_ref, sem).wait()

        def norm(i, c):
            r = pl.multiple_of(i * NORM_ROWS, NORM_ROWS)
            xn_ref[pl.ds(r, NORM_ROWS), :] = _rms_rows(
                stage_ref[pl.ds(r, NORM_ROWS), :], g_ref[...]).astype(xn_ref.dtype)
            return c

        lax.fori_loop(0, rows // NORM_ROWS, norm, 0)

        @pl.when(e + 1 < n_exp)
        def _():
            start_gather(e + 1)

    xn = xn_ref[...]
    gate = jnp.dot(xn, wg_ref[...].astype(BF16), preferred_element_type=F32)
    up = jnp.dot(xn, wu_ref[...].astype(BF16), preferred_element_type=F32)
    o_ref[...] = (_silu(gate) * up).astype(o_ref.dtype)


def expert_up(idx_flat, x, g, w_gate_up, layer, *, rows, d_expert, tf=256):
    t, d = x.shape
    e = w_gate_up.shape[1]
    nf = d_expert // tf
    grid_spec = pltpu.PrefetchScalarGridSpec(
        num_scalar_prefetch=1,
        grid=(e, nf),
        in_specs=[
            pl.BlockSpec(memory_space=pl.ANY),
            pl.BlockSpec((1, d), lambda i, f, idx: (0, 0)),
            pl.BlockSpec((None, None, d, tf), lambda i, f, idx: (layer, i, 0, f)),
            pl.BlockSpec((None, None, d, tf), lambda i, f, idx: (layer, i, 0, nf + f)),
        ],
        out_specs=pl.BlockSpec((None, rows, tf), lambda i, f, idx: (i, 0, f)),
        scratch_shapes=[
            pltpu.VMEM((rows, d), F32),
            pltpu.VMEM((rows, d), BF16),
            pltpu.SemaphoreType.DMA(()),
        ],
    )
    return pl.pallas_call(
        _expert_up_kernel,
        out_shape=jax.ShapeDtypeStruct((e, rows, d_expert), BF16),
        grid_spec=grid_spec,
        compiler_params=_cparams(("arbitrary", "arbitrary")),
        name="expert_up",
    )(idx_flat, x, g.reshape(1, d).astype(F32), w_gate_up, w_gate_up)


COMBINE_CHUNK = 512
COMBINE_UNROLL = 16


def _down_combine_kernel(idx_ref, act_ref, w_ref, gate_ref, x_hbm, o_hbm, acc_ref, y3_ref, stage_ref,
                         sem_in, sem_out, *, seq, cap, n_batch):
    n = pl.program_id(0)
    b = pl.program_id(1)
    e = pl.program_id(2)
    n_exp = pl.num_programs(2)
    cn = stage_ref.shape[2]
    ch = stage_ref.shape[1]
    n_chunks = seq // ch
    col = pl.multiple_of(n * cn, cn)

    def x_copy(c):
        row = pl.multiple_of(b * seq + c * ch, ch)
        return pltpu.make_async_copy(x_hbm.at[pl.ds(row, ch), pl.ds(col, cn)], stage_ref.at[c % 2],
                                     sem_in.at[c % 2])

    def o_copy(c):
        row = pl.multiple_of(b * seq + c * ch, ch)
        return pltpu.make_async_copy(stage_ref.at[c % 2], o_hbm.at[pl.ds(row, ch), pl.ds(col, cn)],
                                     sem_out.at[c % 2])

    @pl.when(e == 0)
    def _():
        x_copy(0).start()
        for c in range(n_chunks):
            x_copy(c).wait()
            if c + 1 < n_chunks:
                x_copy(c + 1).start()
            acc_ref[c * ch:(c + 1) * ch] = stage_ref[c % 2].reshape(ch, SUBLANE, LANE)

    y = jnp.dot(act_ref[...], w_ref[...].astype(BF16), preferred_element_type=F32) * gate_ref[...]
    y3_ref[...] = y.reshape(cap, SUBLANE, LANE)
    base = (e * n_batch + b) * cap

    def body(i, c):
        toks = [idx_ref[base + COMBINE_UNROLL * i + u] for u in range(COMBINE_UNROLL)]
        vals = [acc_ref[toks[u]] + y3_ref[COMBINE_UNROLL * i + u] for u in range(COMBINE_UNROLL)]
        for u in range(COMBINE_UNROLL):
            acc_ref[toks[u]] = vals[u]
        return c

    lax.fori_loop(0, cap // COMBINE_UNROLL, body, 0)

    @pl.when(e == n_exp - 1)
    def _():
        for c in range(n_chunks):
            if c >= 2:
                o_copy(c - 2).wait()
            stage_ref[c % 2] = acc_ref[c * ch:(c + 1) * ch].reshape(ch, cn)
            o_copy(c).start()
        for c in range(max(n_chunks - 2, 0), n_chunks):
            o_copy(c).wait()


def expert_down_combine(idx_local, act, w_down, gates, x, layer, *, batch, seq, cap):
    e, rows, f = act.shape
    t, d = x.shape
    cn = SUBLANE * LANE
    grid_spec = pltpu.PrefetchScalarGridSpec(
        num_scalar_prefetch=1,
        grid=(d // cn, batch, e),
        in_specs=[
            pl.BlockSpec((None, cap, f), lambda n, b, i, idx: (i, b, 0)),
            pl.BlockSpec((None, None, f, cn), lambda n, b, i, idx: (layer, i, 0, n)),
            pl.BlockSpec((None, cap, 1), lambda n, b, i, idx: (i, b, 0)),
            pl.BlockSpec(memory_space=pl.ANY),
        ],
        out_specs=pl.BlockSpec(memory_space=pl.ANY),
        scratch_shapes=[
            pltpu.VMEM((seq, SUBLANE, LANE), F32),
            pltpu.VMEM((cap, SUBLANE, LANE), F32),
            pltpu.VMEM((2, min(COMBINE_CHUNK, seq), cn), F32),
            pltpu.SemaphoreType.DMA((2,)),
            pltpu.SemaphoreType.DMA((2,)),
        ],
    )
    return pl.pallas_call(
        functools.partial(_down_combine_kernel, seq=seq, cap=cap, n_batch=batch),
        out_shape=jax.ShapeDtypeStruct((t, d), F32),
        grid_spec=grid_spec,
        compiler_params=_cparams(("arbitrary", "arbitrary", "arbitrary")),
        name="expert_down_combine",
    )(idx_local, act, w_down, gates, x)


def _final_norm_kernel(x_ref, g_ref, o_ref):
    o_ref[...] = _rms_rows(x_ref[...], g_ref[...])


def final_norm(x, g, *, tm=512):
    t, d = x.shape
    return pl.pallas_call(
        _final_norm_kernel,
        out_shape=jax.ShapeDtypeStruct((t, d), F32),
        grid=(t // tm,),
        in_specs=[pl.BlockSpec((tm, d), lambda i: (i, 0)), pl.BlockSpec((1, d), lambda i: (0, 0))],
        out_specs=pl.BlockSpec((tm, d), lambda i: (i, 0)),
        compiler_params=_cparams(("parallel",)),
        name="final_norm",
    )(x, g.reshape(1, d).astype(F32))


def kernel(x, mem, norm_mix_g, w_in, conv_w, conv_b, dt_bias, a_log, d_skip, ssd_norm_g, gmlp_norm_g,
           gmlp_ws, gmlp_bs, w_out, norm_xattn_g, norm_mem_g, w_q, w_kv, w_o, norm_moe_g, w_router,
           w_gate_up, w_down, final_norm_g):
    batch, seq, d_model = x.shape
    mem_len = mem.shape[1]
    depth = w_in.shape[0]
    n_heads = dt_bias.shape[2]
    d_ssd = n_heads * SSD_HEAD_DIM
    conv_ch = conv_w.shape[2]
    d_gmlp = gmlp_norm_g.shape[1]
    d_expert = w_down.shape[2]
    n_experts = w_router.shape[2]
    cap = CAPACITY_FACTOR * seq // n_experts
    t = batch * seq
    o_dt = d_ssd + conv_ch
    o_u = o_dt + 2 * n_heads

    xf = x.reshape(t, d_model)
    memf = mem.reshape(batch * mem_len, d_model)
    w_in_t = jnp.swapaxes(w_in, 1, 2).reshape(depth * w_in.shape[2], d_model)
    for l in range(depth):
        h = norm_cast(xf, norm_mix_g[l])
        r0 = l * w_in.shape[2]
        proj = matmul_nt(h, w_in_t, r0, o_dt)
        proj_uv = matmul_nt(h, w_in_t, r0 + o_u, 2 * d_gmlp)
        dt_raw = matmul_nt(h, w_in_t, r0 + o_dt, LANE, out_dtype=F32)
        xbc = conv_silu(proj, conv_w[l], conv_b[l], batch=batch, seq=seq, col0=d_ssd)
        y_f, y_b = ssd_scan(xbc, dt_raw, dt_bias[l], a_log[l], batch=batch, seq=seq, d_ssd=d_ssd)
        y_cat = mixer_out(proj, proj_uv, xbc, y_f, y_b, d_skip[l], ssd_norm_g[l], gmlp_norm_g[l],
                          gmlp_ws[l], gmlp_bs[l], d_ssd=d_ssd, d_gmlp=d_gmlp)
        xf, xg, ss = matmul(y_cat, w_out, l, res=xf, next_gain=norm_xattn_g[l])

        q = matmul(xg, w_q, l, row_ss=ss)
        kv = matmul(norm_cast(memf, norm_mem_g[l]), w_kv, l)
        o = cross_attention(q, kv, batch=batch, seq=seq, mem_len=mem_len)
        xf = matmul(o, w_o, l, res=xf)

        aff_t, aff_n = router_affinity(xf, norm_moe_g[l], w_router, l, batch=batch, seq=seq)
        idx, gate = route(aff_t, aff_n, cap=cap)
        idx_local = jnp.swapaxes(idx, 0, 1)
        idx_flat = (idx_local + (jnp.arange(batch, dtype=jnp.int32) * seq)[None, :, None]).reshape(-1)
        gates = jnp.swapaxes(gate, 0, 1).reshape(n_experts, batch * cap, 1)
        act = expert_up(idx_flat, xf, norm_moe_g[l], w_gate_up, l, rows=batch * cap, d_expert=d_expert)
        xf = expert_down_combine(idx_local.reshape(-1), act, w_down, gates, xf, l, batch=batch, seq=seq,
                                 cap=cap)
    out = final_norm(xf, final_norm_g)
    return out.reshape(batch, seq, d_model)
```

```python
import functools

import jax
import jax.numpy as jnp
from jax import lax
from jax.experimental import pallas as pl
from jax.experimental.pallas import tpu as pltpu

F32 = jnp.float32
BF16 = jnp.bfloat16

RMS_EPS = 1e-6
SSD_HEAD_DIM = 64
SSD_GROUPS = 8
D_STATE = 128
D_CONV = 5
CHUNK = 128
GMLP_GROUP_WIDTH = 128
XATTN_HEADS = 4
N_EXPERTS = 16
CAPACITY_FACTOR = 2
LANE = 128
SUBLANE = 8
VMEM_LIMIT = 56 * 1024 * 1024


def _cparams(sem, vmem=VMEM_LIMIT):
    return pltpu.CompilerParams(dimension_semantics=sem, vmem_limit_bytes=vmem)


def _silu(x):
    return x * (1.0 / (1.0 + jnp.exp(-x)))


def _rms_rows(x, g):
    ms = jnp.mean(x * x, axis=-1, keepdims=True)
    return x * lax.rsqrt(ms + RMS_EPS) * g


def _norm_cast_kernel(x_ref, g_ref, o_ref):
    o_ref[...] = _rms_rows(x_ref[...], g_ref[...]).astype(o_ref.dtype)


def norm_cast(x, g, *, tm=512):
    m, k = x.shape
    tm = min(tm, m)
    return pl.pallas_call(
        _norm_cast_kernel,
        out_shape=jax.ShapeDtypeStruct((m, k), BF16),
        grid=(m // tm,),
        in_specs=[pl.BlockSpec((tm, k), lambda i: (i, 0)), pl.BlockSpec((1, k), lambda i: (0, 0))],
        out_specs=pl.BlockSpec((tm, k), lambda i: (i, 0)),
        compiler_params=_cparams(("parallel",)),
        name="norm_cast",
    )(x, g.reshape(1, k).astype(F32))


def _mm_kernel(a_ref, w_ref, o_ref):
    o_ref[...] = jnp.dot(a_ref[...], w_ref[...].astype(BF16),
                         preferred_element_type=F32).astype(o_ref.dtype)


def _mm_nt_kernel(a_ref, wt_ref, o_ref):
    o_ref[...] = lax.dot_general(a_ref[...], wt_ref[...].astype(BF16), (((1,), (1,)), ((), ())),
                                 preferred_element_type=F32).astype(o_ref.dtype)


def matmul_nt(a, wt, row0, n, *, out_dtype=BF16, tm=1024, tn=512):
    m, k = a.shape
    tm = min(tm, m)
    tn = min(tn, n)
    assert m % tm == 0 and n % tn == 0 and row0 % 8 == 0 and wt.shape[1] == k
    return pl.pallas_call(
        _mm_nt_kernel,
        out_shape=jax.ShapeDtypeStruct((m, n), out_dtype),
        grid=(m // tm, n // tn),
        in_specs=[pl.BlockSpec((tm, k), lambda i, j: (i, 0)),
                  pl.BlockSpec((pl.Element(tn), pl.Element(k)),
                               lambda i, j: (pl.multiple_of(row0 + j * tn, 8), 0))],
        out_specs=pl.BlockSpec((tm, tn), lambda i, j: (i, j)),
        compiler_params=_cparams(("parallel", "arbitrary")),
        name="matmul_nt",
    )(a, wt)


def _mm_res_kernel(a_ref, w_ref, r_ref, o_ref):
    o_ref[...] = r_ref[...] + jnp.dot(a_ref[...], w_ref[...].astype(BF16), preferred_element_type=F32)


def _mm_res_norm_kernel(a_ref, w_ref, r_ref, g_ref, o_ref, xg_ref, ss_ref):
    x = r_ref[...] + jnp.dot(a_ref[...], w_ref[...].astype(BF16), preferred_element_type=F32)
    o_ref[...] = x
    xg_ref[...] = (x * g_ref[...]).astype(xg_ref.dtype)

    @pl.when(pl.program_id(1) == 0)
    def _():
        ss_ref[...] = jnp.zeros_like(ss_ref)

    ss_ref[...] += jnp.sum(x * x, axis=1, keepdims=True)


def _mm_rowscale_kernel(a_ref, w_ref, ss_ref, o_ref):
    inv = lax.rsqrt(ss_ref[...] * (1.0 / a_ref.shape[1]) + RMS_EPS)
    o_ref[...] = (jnp.dot(a_ref[...], w_ref[...].astype(BF16), preferred_element_type=F32)
                  * inv).astype(o_ref.dtype)


def matmul(a, w, layer, *, n_cols=None, res=None, out_dtype=BF16, tm=1024, tn=512, next_gain=None,
           row_ss=None):
    m, k = a.shape
    n = w.shape[2] if n_cols is None else n_cols
    tm = min(tm, m)
    tn = min(tn, n)
    assert m % tm == 0 and n % tn == 0
    a_spec = pl.BlockSpec((tm, k), lambda i, j: (i, 0))
    w_spec = pl.BlockSpec((None, k, tn), lambda i, j: (layer, 0, j))
    o_spec = pl.BlockSpec((tm, tn), lambda i, j: (i, j))
    if next_gain is not None:
        return pl.pallas_call(
            _mm_res_norm_kernel,
            out_shape=(jax.ShapeDtypeStruct((m, n), F32), jax.ShapeDtypeStruct((m, n), BF16),
                       jax.ShapeDtypeStruct((m, 1), F32)),
            grid=(m // tm, n // tn),
            in_specs=[a_spec, w_spec, o_spec, pl.BlockSpec((1, tn), lambda i, j: (0, j))],
            out_specs=(o_spec, o_spec, pl.BlockSpec((tm, 1), lambda i, j: (i, 0))),
            compiler_params=_cparams(("parallel", "arbitrary")),
            name="matmul_residual_norm",
        )(a, w, res, next_gain.reshape(1, n).astype(F32))
    if row_ss is not None:
        return pl.pallas_call(
            _mm_rowscale_kernel,
            out_shape=jax.ShapeDtypeStruct((m, n), out_dtype),
            grid=(m // tm, n // tn),
            in_specs=[a_spec, w_spec, pl.BlockSpec((tm, 1), lambda i, j: (i, 0))],
            out_specs=o_spec,
            compiler_params=_cparams(("parallel", "arbitrary")),
            name="matmul_rowscale",
        )(a, w, row_ss)
    if res is None:
        return pl.pallas_call(
            _mm_kernel,
            out_shape=jax.ShapeDtypeStruct((m, n), out_dtype),
            grid=(m // tm, n // tn),
            in_specs=[a_spec, w_spec],
            out_specs=o_spec,
            compiler_params=_cparams(("parallel", "arbitrary")),
            name="matmul",
        )(a, w)
    return pl.pallas_call(
        _mm_res_kernel,
        out_shape=jax.ShapeDtypeStruct((m, n), F32),
        grid=(m // tm, n // tn),
        in_specs=[a_spec, w_spec, o_spec],
        out_specs=o_spec,
        compiler_params=_cparams(("parallel", "arbitrary")),
        name="matmul_residual",
    )(a, w, res)


HALO = 16


def _conv_kernel(prev_ref, cur_ref, next_ref, w_ref, b_ref, o_ref, ext_ref, *, n_seq_tiles):
    ts = cur_ref.shape[0]
    s = pl.program_id(1)
    prev = prev_ref[...].astype(F32)
    nxt = next_ref[...].astype(F32)
    ext_ref[0:HALO, :] = jnp.where(s == 0, 0.0, prev)
    ext_ref[HALO:HALO + ts, :] = cur_ref[...].astype(F32)
    ext_ref[HALO + ts:, :] = jnp.where(s == n_seq_tiles - 1, 0.0, nxt)
    acc = jnp.zeros(cur_ref.shape, F32) + b_ref[...]
    for k in range(D_CONV):
        off = HALO - D_CONV // 2 + k
        acc = acc + ext_ref[off:off + ts, :] * w_ref[k:k + 1, :]
    o_ref[...] = _silu(acc).astype(o_ref.dtype)


def conv_silu(proj, conv_w, conv_b, *, batch, seq, col0, ts=512, tc=1024):
    c = conv_w.shape[1]
    t = batch * seq
    nst = seq // ts
    hb = ts // HALO
    cb0 = col0 // tc
    assert col0 % tc == 0 and c % tc == 0 and seq % ts == 0
    last_halo = t // HALO - 1

    def prev_map(b, s, j):
        return (jnp.maximum((b * nst + s) * hb - 1, 0), cb0 + j)

    def next_map(b, s, j):
        return (jnp.minimum((b * nst + s + 1) * hb, last_halo), cb0 + j)

    return pl.pallas_call(
        functools.partial(_conv_kernel, n_seq_tiles=nst),
        out_shape=jax.ShapeDtypeStruct((t, c), BF16),
        grid=(batch, nst, c // tc),
        in_specs=[
            pl.BlockSpec((HALO, tc), prev_map),
            pl.BlockSpec((ts, tc), lambda b, s, j: (b * nst + s, cb0 + j)),
            pl.BlockSpec((HALO, tc), next_map),
            pl.BlockSpec((D_CONV, tc), lambda b, s, j: (0, j)),
            pl.BlockSpec((1, tc), lambda b, s, j: (0, j)),
        ],
        out_specs=pl.BlockSpec((ts, tc), lambda b, s, j: (b * nst + s, j)),
        scratch_shapes=[pltpu.VMEM((ts + 2 * HALO, tc), F32)],
        compiler_params=_cparams(("parallel", "parallel", "parallel")),
        name="conv_silu",
    )(proj, proj, proj, conv_w.astype(F32), conv_b.reshape(1, c).astype(F32))


def _ssd_direction(xs_ref, b_ref, c_ref, dt_ref, bias_ref, alog_ref, expand_ref, st_ref, y_ref,
                   *, reverse, col0, n_heads):
    L = CHUNK
    hp = SSD_HEAD_DIM
    heads_per_group = n_heads // SSD_GROUPS
    gw = heads_per_group * hp
    row = lax.broadcasted_iota(jnp.int32, (L, L), 0)
    col = lax.broadcasted_iota(jnp.int32, (L, L), 1)
    keep = (col >= row) if reverse else (col <= row)
    tri = jnp.where(keep, 1.0, 0.0).astype(F32)

    xraw = dt_ref[...] + bias_ref[...]
    dt = jnp.maximum(xraw, 0.0) + jnp.log1p(jnp.exp(-jnp.abs(xraw)))
    a = -jnp.exp(alog_ref[...])
    dta = dt * a
    acs = jnp.dot(tri, dta, preferred_element_type=F32, precision=lax.Precision.HIGHEST)
    acs_t = acs.T
    total = acs[0:1, :] if reverse else acs[L - 1:L, :]
    dte = jnp.exp(total - acs)
    ea = jnp.exp(acs)
    cdec = jnp.broadcast_to(jnp.exp(total), (8, LANE))

    stack = jnp.concatenate([dt, dte, ea, cdec], axis=0).astype(BF16)
    ex = jnp.dot(stack, expand_ref[...], preferred_element_type=F32)
    dt_x = ex[0:L]
    dte_x = ex[L:2 * L]
    ea_x = ex[2 * L:3 * L]
    cdec_x = ex[3 * L:3 * L + 1]

    lane_head = lax.broadcasted_iota(jnp.int32, (L, gw), 1) // hp
    for g in range(SSD_GROUPS):
        sl = slice(g * gw, (g + 1) * gw)
        xs_g = xs_ref[:, sl].astype(F32)
        xdt = xs_g * dt_x[:, sl]
        xdt_bf = xdt.astype(BF16)
        xdtw_bf = (xdt * dte_x[:, sl]).astype(BF16)
        b_g = b_ref[:, g * D_STATE:(g + 1) * D_STATE]
        c_g = c_ref[:, g * D_STATE:(g + 1) * D_STATE]
        cb = lax.dot_general(c_g, b_g, (((1,), (1,)), ((), ())), preferred_element_type=F32)
        y_g = jnp.zeros((L, gw), F32)
        for r in range(heads_per_group):
            hc = col0 + g * heads_per_group + r
            seg = acs[:, hc:hc + 1] - acs_t[hc:hc + 1, :]
            decay = jnp.exp(jnp.where(keep, seg, -jnp.inf))
            m = (cb * decay).astype(BF16)
            x_r = jnp.where(lane_head == r, xdt_bf, jnp.zeros_like(xdt_bf))
            y_g = y_g + jnp.dot(m, x_r, preferred_element_type=F32)
        st = st_ref[g]
        y_off = jnp.dot(c_g, st.astype(BF16), preferred_element_type=F32) * ea_x[:, sl]
        y_ref[:, sl] = (y_g + y_off).astype(y_ref.dtype)
        b_t = b_g.astype(F32).T.astype(BF16)
        st_ref[g] = st * cdec_x[:, sl] + jnp.dot(b_t, xdtw_bf, preferred_element_type=F32)


def _ssd_kernel(xs_f, b_f, c_f, dt_f, xs_b, b_b, c_b, dt_b, bias_ref, alog_ref, expand_ref,
                yf_ref, yb_ref, stf_ref, stb_ref, *, n_heads):
    @pl.when(pl.program_id(1) == 0)
    def _():
        stf_ref[...] = jnp.zeros_like(stf_ref)
        stb_ref[...] = jnp.zeros_like(stb_ref)

    _ssd_direction(xs_f, b_f, c_f, dt_f, bias_ref, alog_ref, expand_ref.at[0], stf_ref, yf_ref,
                   reverse=False, col0=0, n_heads=n_heads)
    _ssd_direction(xs_b, b_b, c_b, dt_b, bias_ref, alog_ref, expand_ref.at[1], stb_ref, yb_ref,
                   reverse=True, col0=n_heads, n_heads=n_heads)


def ssd_scan(xbc, dt_raw, dt_bias, a_log, *, batch, seq, d_ssd):
    t = batch * seq
    nc = seq // CHUNK
    n_heads = d_ssd // SSD_HEAD_DIM
    gn = SSD_GROUPS * D_STATE
    assert d_ssd % gn == 0 and 2 * n_heads <= LANE
    xb = d_ssd // gn
    bias = jnp.zeros((1, LANE), F32).at[0, :2 * n_heads].set(dt_bias.reshape(-1).astype(F32))
    alog = jnp.zeros((1, LANE), F32).at[0, :2 * n_heads].set(a_log.reshape(-1).astype(F32))
    lane_h = jnp.arange(d_ssd) // SSD_HEAD_DIM
    expand = jnp.stack([
        (jnp.arange(LANE)[:, None] == (d * n_heads + lane_h)[None, :]) for d in range(2)
    ]).astype(BF16)

    def fwd(b, c):
        return b * nc + c

    def bwd(b, c):
        return b * nc + (nc - 1 - c)

    def specs(rmap):
        return [
            pl.BlockSpec((CHUNK, d_ssd), lambda b, c: (rmap(b, c), 0)),
            pl.BlockSpec((CHUNK, gn), lambda b, c: (rmap(b, c), xb)),
            pl.BlockSpec((CHUNK, gn), lambda b, c: (rmap(b, c), xb + 1)),
            pl.BlockSpec((CHUNK, LANE), lambda b, c: (rmap(b, c), 0)),
        ]

    const2 = lambda b, c: (0, 0)
    gw = d_ssd // SSD_GROUPS
    return pl.pallas_call(
        functools.partial(_ssd_kernel, n_heads=n_heads),
        out_shape=(jax.ShapeDtypeStruct((t, d_ssd), BF16), jax.ShapeDtypeStruct((t, d_ssd), BF16)),
        grid=(batch, nc),
        in_specs=specs(fwd) + specs(bwd) + [
            pl.BlockSpec((1, LANE), const2),
            pl.BlockSpec((1, LANE), const2),
            pl.BlockSpec((2, LANE, d_ssd), lambda b, c: (0, 0, 0)),
        ],
        out_specs=(
            pl.BlockSpec((CHUNK, d_ssd), lambda b, c: (fwd(b, c), 0)),
            pl.BlockSpec((CHUNK, d_ssd), lambda b, c: (bwd(b, c), 0)),
        ),
        scratch_shapes=[pltpu.VMEM((SSD_GROUPS, D_STATE, gw), F32),
                        pltpu.VMEM((SSD_GROUPS, D_STATE, gw), F32)],
        compiler_params=_cparams(("parallel", "arbitrary")),
        name="ssd_scan",
    )(xbc, xbc, xbc, dt_raw, xbc, xbc, xbc, dt_raw, bias, alog, expand)


def _gelu(x):
    return 0.5 * x * (1.0 + lax.erf(x * (2.0 ** -0.5)))


def _mixer_out_kernel(z_ref, xs_ref, yf_ref, yb_ref, u_ref, v_ref, dskip_ref, sg_ref, gg_ref,
                      ws_ref, bs_ref, o_ref, *, d_ssd):
    gw = d_ssd // SSD_GROUPS
    y = (yf_ref[...].astype(F32) + yb_ref[...].astype(F32)
         + xs_ref[...].astype(F32) * dskip_ref[...]) * _silu(z_ref[...].astype(F32))
    for g in range(SSD_GROUPS):
        sl = slice(g * gw, (g + 1) * gw)
        o_ref[:, sl] = _rms_rows(y[:, sl], sg_ref[:, sl]).astype(o_ref.dtype)

    uu = _gelu(u_ref[...].astype(F32))
    vn = _rms_rows(_gelu(v_ref[...].astype(F32)), gg_ref[...]).astype(BF16)
    bs = bs_ref[...]
    for g in range(ws_ref.shape[0]):
        sl = slice(g * GMLP_GROUP_WIDTH, (g + 1) * GMLP_GROUP_WIDTH)
        sp = jnp.dot(ws_ref[g], vn[:, sl], preferred_element_type=F32) + bs[:, g:g + 1]
        o_ref[:, d_ssd + g * GMLP_GROUP_WIDTH:d_ssd + (g + 1) * GMLP_GROUP_WIDTH] = (
            uu[:, sl] * sp).astype(o_ref.dtype)


def mixer_out(proj, proj_uv, xbc, y_f, y_b, d_skip, ssd_norm_g, gmlp_norm_g, gmlp_ws, gmlp_bs, *, d_ssd,
              d_gmlp):
    t = proj.shape[0]
    assert d_ssd == d_gmlp
    w = d_ssd
    n_groups = gmlp_ws.shape[0]
    dskip = jnp.repeat(d_skip.astype(F32), SSD_HEAD_DIM).reshape(1, d_ssd)
    row = lambda i: (i, 0)
    const = lambda i: (0, 0)
    return pl.pallas_call(
        functools.partial(_mixer_out_kernel, d_ssd=d_ssd),
        out_shape=jax.ShapeDtypeStruct((t, d_ssd + d_gmlp), BF16),
        grid=(t // CHUNK,),
        in_specs=[
            pl.BlockSpec((CHUNK, w), row),
            pl.BlockSpec((CHUNK, w), row),
            pl.BlockSpec((CHUNK, w), row),
            pl.BlockSpec((CHUNK, w), row),
            pl.BlockSpec((CHUNK, w), row),
            pl.BlockSpec((CHUNK, w), lambda i: (i, 1)),
            pl.BlockSpec((1, w), const),
            pl.BlockSpec((1, w), const),
            pl.BlockSpec((1, w), const),
            pl.BlockSpec((n_groups, CHUNK, CHUNK), lambda i: (0, 0, 0)),
            pl.BlockSpec((CHUNK, n_groups), const),
        ],
        out_specs=pl.BlockSpec((CHUNK, d_ssd + d_gmlp), row),
        compiler_params=_cparams(("parallel",)),
        name="mixer_out",
    )(proj, xbc, y_f, y_b, proj_uv, proj_uv, dskip, ssd_norm_g.reshape(1, -1).astype(F32),
      gmlp_norm_g.reshape(1, -1).astype(F32), gmlp_ws.astype(BF16), gmlp_bs.T.astype(F32))


def _xattn_kernel(q_ref, k_ref, v_ref, o_ref, *, scale):
    s = lax.dot_general(q_ref[...], k_ref[...], (((1,), (1,)), ((), ())),
                        preferred_element_type=F32) * scale
    p = jnp.exp(s - jnp.max(s, axis=-1, keepdims=True))
    denom = jnp.sum(p, axis=-1, keepdims=True)
    o = jnp.dot(p.astype(BF16), v_ref[...], preferred_element_type=F32)
    o_ref[...] = (o / denom).astype(o_ref.dtype)


def cross_attention(q, kv, *, batch, seq, mem_len, tq=1024):
    t, d = q.shape
    hd = d // XATTN_HEADS
    nq = seq // tq
    return pl.pallas_call(
        functools.partial(_xattn_kernel, scale=hd ** -0.5),
        out_shape=jax.ShapeDtypeStruct((t, d), BF16),
        grid=(batch, XATTN_HEADS, nq),
        in_specs=[
            pl.BlockSpec((tq, hd), lambda b, h, i: (b * nq + i, h)),
            pl.BlockSpec((mem_len, hd), lambda b, h, i: (b, h)),
            pl.BlockSpec((mem_len, hd), lambda b, h, i: (b, XATTN_HEADS + h)),
        ],
        out_specs=pl.BlockSpec((tq, hd), lambda b, h, i: (b * nq + i, h)),
        compiler_params=_cparams(("parallel", "parallel", "parallel")),
        name="cross_attention",
    )(q, kv, kv)


def _router_kernel(x_ref, g_ref, whi_ref, wlo_ref, afft_ref, affn_ref, *, n_experts):
    h = _rms_rows(x_ref[...], g_ref[...])
    h_hi = h.astype(BF16)
    h_lo = (h - h_hi.astype(F32)).astype(BF16)
    logits = (jnp.dot(h_hi, whi_ref[...], preferred_element_type=F32)
              + jnp.dot(h_lo, whi_ref[...], preferred_element_type=F32)
              + jnp.dot(h_hi, wlo_ref[...], preferred_element_type=F32))
    lane = lax.broadcasted_iota(jnp.int32, logits.shape, 1)
    valid = lane < n_experts
    lm = jnp.where(valid, logits, -jnp.inf)
    q = jnp.where(valid, jnp.exp(lm - jnp.max(lm, axis=1, keepdims=True)), 0.0)
    aff = q / jnp.sum(q, axis=1, keepdims=True)
    affn_ref[...] = aff
    afft_ref[...] = aff.T[0:n_experts, :]


def router_affinity(x, g, w_router, layer, *, batch, seq, tm=512):
    t, d = x.shape
    e = w_router.shape[2]
    w_pad = jnp.zeros((d, LANE), F32).at[:, :e].set(w_router[layer].astype(F32))
    w_hi = w_pad.astype(BF16)
    w_lo = (w_pad - w_hi.astype(F32)).astype(BF16)
    ns = seq // tm
    return pl.pallas_call(
        functools.partial(_router_kernel, n_experts=e),
        out_shape=(jax.ShapeDtypeStruct((batch, e, seq), F32), jax.ShapeDtypeStruct((t, LANE), F32)),
        grid=(batch, ns),
        in_specs=[
            pl.BlockSpec((tm, d), lambda b, i: (b * ns + i, 0)),
            pl.BlockSpec((1, d), lambda b, i: (0, 0)),
            pl.BlockSpec((d, LANE), lambda b, i: (0, 0)),
            pl.BlockSpec((d, LANE), lambda b, i: (0, 0)),
        ],
        out_specs=(pl.BlockSpec((None, e, tm), lambda b, i: (b, 0, i)),
                   pl.BlockSpec((tm, LANE), lambda b, i: (b * ns + i, 0))),
        compiler_params=_cparams(("parallel", "parallel")),
        name="router",
    )(x, g.reshape(1, d).astype(F32), w_hi, w_lo)


BISECT_ITERS = 152


def _route_kernel(afft_ref, affn_ref, idx_ref, gate_ref, slot_ref, *, cap):
    n_exp, seq = afft_ref.shape
    blk = LANE
    nblk = seq // blk
    capf = float(cap)

    xt = afft_ref[...]

    def bisect(_, carry):
        lo, hi = carry
        mid = 0.5 * (lo + hi)
        cnt = jnp.sum(jnp.where(xt >= mid, 1.0, 0.0), axis=1, keepdims=True)
        ge = cnt >= capf
        return jnp.where(ge, mid, lo), jnp.where(ge, hi, mid)

    lo, hi = lax.fori_loop(0, BISECT_ITERS, bisect,
                           (jnp.zeros((n_exp, 1), F32), jnp.full((n_exp, 1), 2.0, F32)))
    r_i = lax.broadcasted_iota(jnp.int32, (n_exp, LANE), 0)
    c_i = lax.broadcasted_iota(jnp.int32, (n_exp, LANE), 1)
    diag = r_i == c_i
    lane_ok = lax.broadcasted_iota(jnp.int32, (1, LANE), 1) < n_exp
    lo_r = jnp.where(lane_ok, jnp.sum(jnp.where(diag, lo, 0.0), axis=0, keepdims=True), 4.0)
    hi_r = jnp.where(lane_ok, jnp.sum(jnp.where(diag, hi, 0.0), axis=0, keepdims=True), 4.0)

    n_gt = jnp.sum(jnp.where(affn_ref[...] >= hi_r, 1.0, 0.0), axis=0, keepdims=True)
    need = capf - n_gt

    tr = lax.broadcasted_iota(jnp.int32, (blk, blk), 0)
    tc = lax.broadcasted_iota(jnp.int32, (blk, blk), 1)
    tril = jnp.where(tr >= tc, 1.0, 0.0).astype(BF16)
    carry_eq = jnp.zeros((1, LANE), F32)
    carry_sel = jnp.zeros((1, LANE), F32)
    for k in range(nblk):
        x = affn_ref[k * blk:(k + 1) * blk, :]
        gt = x >= hi_r
        eq = jnp.logical_and(x >= lo_r, jnp.logical_not(gt))
        pos_eq = jnp.dot(tril, jnp.where(eq, 1.0, 0.0).astype(BF16), preferred_element_type=F32) + carry_eq
        carry_eq = pos_eq[blk - 1:blk, :]
        sel = jnp.logical_or(gt, jnp.logical_and(eq, pos_eq <= need))
        cs = jnp.dot(tril, jnp.where(sel, 1.0, 0.0).astype(BF16), preferred_element_type=F32) + carry_sel
        carry_sel = cs[blk - 1:blk, :]
        slot_ref[k * blk:(k + 1) * blk, :] = jnp.where(sel, cs - 1.0, -1.0)

    s_iota = lax.broadcasted_iota(jnp.int32, (blk, cap), 1).astype(F32)
    t_iota = lax.broadcasted_iota(jnp.int32, (blk, 1), 0).astype(F32)
    for e in range(n_exp):
        def body(k, acc):
            acc_i, acc_g = acc
            r0 = pl.multiple_of(k * blk, blk)
            slot = slot_ref[pl.ds(r0, blk), :][:, e:e + 1]
            aff = affn_ref[pl.ds(r0, blk), :][:, e:e + 1]
            hit = jnp.broadcast_to(slot, (blk, cap)) == s_iota
            tok = t_iota + lax.convert_element_type(k * blk, F32)
            acc_i = acc_i + jnp.sum(jnp.where(hit, tok, 0.0), axis=0, keepdims=True)
            acc_g = acc_g + jnp.sum(jnp.where(hit, aff, 0.0), axis=0, keepdims=True)
            return acc_i, acc_g

        acc_i, acc_g = lax.fori_loop(0, nblk, body, (jnp.zeros((1, cap), F32), jnp.zeros((1, cap), F32)))
        idx_ref[e:e + 1, :] = acc_i.astype(jnp.int32)
        gate_ref[e:e + 1, :] = acc_g


def route(aff_t, aff_n, *, cap):
    batch, e, seq = aff_t.shape
    return pl.pallas_call(
        functools.partial(_route_kernel, cap=cap),
        out_shape=(jax.ShapeDtypeStruct((batch, e, cap), jnp.int32),
                   jax.ShapeDtypeStruct((batch, e, cap), F32)),
        grid=(batch,),
        in_specs=[pl.BlockSpec((None, e, seq), lambda b: (b, 0, 0)),
                  pl.BlockSpec((seq, LANE), lambda b: (b, 0))],
        out_specs=(pl.BlockSpec((None, e, cap), lambda b: (b, 0, 0)),
                   pl.BlockSpec((None, e, cap), lambda b: (b, 0, 0))),
        scratch_shapes=[pltpu.VMEM((seq, LANE), F32)],
        compiler_params=_cparams(("parallel",)),
        name="route",
    )(aff_t, aff_n)


def _row_copy(src_ref, src_row, dst_ref, dst_row, sem):
    return pltpu.make_async_copy(src_ref.at[pl.ds(src_row, 1)], dst_ref.at[pl.ds(dst_row, 1)], sem)


ROW_UNROLL = 16
NORM_ROWS = 128


def _expert_up_kernel(idx_ref, x_hbm, g_ref, wg_ref, wu_ref, o_ref, stage_ref, xn_ref, sem):
    e = pl.program_id(0)
    n_exp = pl.num_programs(0)
    rows = xn_ref.shape[0]

    def start_gather(expert):
        def body(j, c):
            for p in range(ROW_UNROLL):
                r = ROW_UNROLL * j + p
                _row_copy(x_hbm, idx_ref[expert * rows + r], stage_ref, r, sem).start(priority=p % 2)
            return c

        lax.fori_loop(0, rows // ROW_UNROLL, body, 0)

    @pl.when(pl.program_id(1) == 0)
    def _():
        @pl.when(e == 0)
        def _():
            start_gather(0)

        pltpu.make_async_copy(x_hbm.at[pl.ds(0, rows)], stage_ref, sem).wait()

        def norm(i, c):
            r = pl.multiple_of(i * NORM_ROWS, NORM_ROWS)
            xn_ref[pl.ds(r, NORM_ROWS), :] = _rms_rows(
                stage_ref[pl.ds(r, NORM_ROWS), :], g_ref[...]).astype(xn_ref.dtype)
            return c

        lax.fori_loop(0, rows // NORM_ROWS, norm, 0)

        @pl.when(e + 1 < n_exp)
        def _():
            start_gather(e + 1)

    xn = xn_ref[...]
    gate = jnp.dot(xn, wg_ref[...].astype(BF16), preferred_element_type=F32)
    up = jnp.dot(xn, wu_ref[...].astype(BF16), preferred_element_type=F32)
    o_ref[...] = (_silu(gate) * up).astype(o_ref.dtype)


def expert_up(idx_flat, x, g, w_gate_up, layer, *, rows, d_expert, tf=256):
    t, d = x.shape
    e = w_gate_up.shape[1]
    nf = d_expert // tf
    grid_spec = pltpu.PrefetchScalarGridSpec(
        num_scalar_prefetch=1,
        grid=(e, nf),
        in_specs=[
            pl.BlockSpec(memory_space=pl.ANY),
            pl.BlockSpec((1, d), lambda i, f, idx: (0, 0)),
            pl.BlockSpec((None, None, d, tf), lambda i, f, idx: (layer, i, 0, f)),
            pl.BlockSpec((None, None, d, tf), lambda i, f, idx: (layer, i, 0, nf + f)),
        ],
        out_specs=pl.BlockSpec((None, rows, tf), lambda i, f, idx: (i, 0, f)),
        scratch_shapes=[
            pltpu.VMEM((rows, d), F32),
            pltpu.VMEM((rows, d), BF16),
            pltpu.SemaphoreType.DMA(()),
        ],
    )
    return pl.pallas_call(
        _expert_up_kernel,
        out_shape=jax.ShapeDtypeStruct((e, rows, d_expert), BF16),
        grid_spec=grid_spec,
        compiler_params=_cparams(("arbitrary", "arbitrary")),
        name="expert_up",
    )(idx_flat, x, g.reshape(1, d).astype(F32), w_gate_up, w_gate_up)


COMBINE_CHUNK = 512
COMBINE_UNROLL = 32


def _down_combine_kernel(idx_ref, act_ref, w_ref, gate_ref, x_hbm, o_hbm, acc_ref, y3_ref, stage_ref,
                         sem_in, sem_out, *, seq, cap, n_batch):
    n = pl.program_id(0)
    b = pl.program_id(1)
    e = pl.program_id(2)
    n_exp = pl.num_programs(2)
    cn = stage_ref.shape[2]
    ch = stage_ref.shape[1]
    n_chunks = seq // ch
    col = pl.multiple_of(n * cn, cn)

    def x_copy(c):
        row = pl.multiple_of(b * seq + c * ch, ch)
        return pltpu.make_async_copy(x_hbm.at[pl.ds(row, ch), pl.ds(col, cn)], stage_ref.at[c % 2],
                                     sem_in.at[c % 2])

    def o_copy(c):
        row = pl.multiple_of(b * seq + c * ch, ch)
        return pltpu.make_async_copy(stage_ref.at[c % 2], o_hbm.at[pl.ds(row, ch), pl.ds(col, cn)],
                                     sem_out.at[c % 2])

    @pl.when(e == 0)
    def _():
        x_copy(0).start()
        for c in range(n_chunks):
            x_copy(c).wait()
            if c + 1 < n_chunks:
                x_copy(c + 1).start()
            acc_ref[c * ch:(c + 1) * ch] = stage_ref[c % 2].reshape(ch, SUBLANE, LANE)

    y = jnp.dot(act_ref[...], w_ref[...].astype(BF16), preferred_element_type=F32) * gate_ref[...]
    y3_ref[...] = y.reshape(cap, SUBLANE, LANE)
    base = (e * n_batch + b) * cap

    def body(i, c):
        toks = [idx_ref[base + COMBINE_UNROLL * i + u] for u in range(COMBINE_UNROLL)]
        vals = [acc_ref[toks[u]] + y3_ref[COMBINE_UNROLL * i + u] for u in range(COMBINE_UNROLL)]
        for u in range(COMBINE_UNROLL):
            acc_ref[toks[u]] = vals[u]
        return c

    lax.fori_loop(0, cap // COMBINE_UNROLL, body, 0)

    @pl.when(e == n_exp - 1)
    def _():
        for c in range(n_chunks):
            if c >= 2:
                o_copy(c - 2).wait()
            stage_ref[c % 2] = acc_ref[c * ch:(c + 1) * ch].reshape(ch, cn)
            o_copy(c).start()
        for c in range(max(n_chunks - 2, 0), n_chunks):
            o_copy(c).wait()


def expert_down_combine(idx_local, act, w_down, gates, x, layer, *, batch, seq, cap):
    e, rows, f = act.shape
    t, d = x.shape
    cn = SUBLANE * LANE
    grid_spec = pltpu.PrefetchScalarGridSpec(
        num_scalar_prefetch=1,
        grid=(d // cn, batch, e),
        in_specs=[
            pl.BlockSpec((None, cap, f), lambda n, b, i, idx: (i, b, 0)),
            pl.BlockSpec((None, None, f, cn), lambda n, b, i, idx: (layer, i, 0, n)),
            pl.BlockSpec((None, cap, 1), lambda n, b, i, idx: (i, b, 0)),
            pl.BlockSpec(memory_space=pl.ANY),
        ],
        out_specs=pl.BlockSpec(memory_space=pl.ANY),
        scratch_shapes=[
            pltpu.VMEM((seq, SUBLANE, LANE), F32),
            pltpu.VMEM((cap, SUBLANE, LANE), F32),
            pltpu.VMEM((2, min(COMBINE_CHUNK, seq), cn), F32),
            pltpu.SemaphoreType.DMA((2,)),
            pltpu.SemaphoreType.DMA((2,)),
        ],
    )
    return pl.pallas_call(
        functools.partial(_down_combine_kernel, seq=seq, cap=cap, n_batch=batch),
        out_shape=jax.ShapeDtypeStruct((t, d), F32),
        grid_spec=grid_spec,
        compiler_params=_cparams(("arbitrary", "arbitrary", "arbitrary")),
        name="expert_down_combine",
    )(idx_local, act, w_down, gates, x)


def _final_norm_kernel(x_ref, g_ref, o_ref):
    o_ref[...] = _rms_rows(x_ref[...], g_ref[...])


def final_norm(x, g, *, tm=512):
    t, d = x.shape
    return pl.pallas_call(
        _final_norm_kernel,
        out_shape=jax.ShapeDtypeStruct((t, d), F32),
        grid=(t // tm,),
        in_specs=[pl.BlockSpec((tm, d), lambda i: (i, 0)), pl.BlockSpec((1, d), lambda i: (0, 0))],
        out_specs=pl.BlockSpec((tm, d), lambda i: (i, 0)),
        compiler_params=_cparams(("parallel",)),
        name="final_norm",
    )(x, g.reshape(1, d).astype(F32))


def kernel(x, mem, norm_mix_g, w_in, conv_w, conv_b, dt_bias, a_log, d_skip, ssd_norm_g, gmlp_norm_g,
           gmlp_ws, gmlp_bs, w_out, norm_xattn_g, norm_mem_g, w_q, w_kv, w_o, norm_moe_g, w_router,
           w_gate_up, w_down, final_norm_g):
    batch, seq, d_model = x.shape
    mem_len = mem.shape[1]
    depth = w_in.shape[0]
    n_heads = dt_bias.shape[2]
    d_ssd = n_heads * SSD_HEAD_DIM
    conv_ch = conv_w.shape[2]
    d_gmlp = gmlp_norm_g.shape[1]
    d_expert = w_down.shape[2]
    n_experts = w_router.shape[2]
    cap = CAPACITY_FACTOR * seq // n_experts
    t = batch * seq
    o_dt = d_ssd + conv_ch
    o_u = o_dt + 2 * n_heads

    xf = x.reshape(t, d_model)
    memf = mem.reshape(batch * mem_len, d_model)
    w_in_t = jnp.swapaxes(w_in, 1, 2).reshape(depth * w_in.shape[2], d_model)
    for l in range(depth):
        h = norm_cast(xf, norm_mix_g[l])
        r0 = l * w_in.shape[2]
        proj = matmul_nt(h, w_in_t, r0, o_dt)
        proj_uv = matmul_nt(h, w_in_t, r0 + o_u, 2 * d_gmlp)
        dt_raw = matmul_nt(h, w_in_t, r0 + o_dt, LANE, out_dtype=F32)
        xbc = conv_silu(proj, conv_w[l], conv_b[l], batch=batch, seq=seq, col0=d_ssd)
        y_f, y_b = ssd_scan(xbc, dt_raw, dt_bias[l], a_log[l], batch=batch, seq=seq, d_ssd=d_ssd)
        y_cat = mixer_out(proj, proj_uv, xbc, y_f, y_b, d_skip[l], ssd_norm_g[l], gmlp_norm_g[l],
                          gmlp_ws[l], gmlp_bs[l], d_ssd=d_ssd, d_gmlp=d_gmlp)
        xf, xg, ss = matmul(y_cat, w_out, l, res=xf, next_gain=norm_xattn_g[l])

        q = matmul(xg, w_q, l, row_ss=ss)
        kv = matmul(norm_cast(memf, norm_mem_g[l]), w_kv, l)
        o = cross_attention(q, kv, batch=batch, seq=seq, mem_len=mem_len)
        xf = matmul(o, w_o, l, res=xf)

        aff_t, aff_n = router_affinity(xf, norm_moe_g[l], w_router, l, batch=batch, seq=seq)
        idx, gate = route(aff_t, aff_n, cap=cap)
        idx_local = jnp.swapaxes(idx, 0, 1)
        idx_flat = (idx_local + (jnp.arange(batch, dtype=jnp.int32) * seq)[None, :, None]).reshape(-1)
        gates = jnp.swapaxes(gate, 0, 1).reshape(n_experts, batch * cap, 1)
        act = expert_up(idx_flat, xf, norm_moe_g[l], w_gate_up, l, rows=batch * cap, d_expert=d_expert)
        xf = expert_down_combine(idx_local.reshape(-1), act, w_down, gates, xf, l, batch=batch, seq=seq,
                                 cap=cap)
    out = final_norm(xf, final_norm_g)
    return out.reshape(batch, seq, d_model)
```
